```python
import jax, jax.numpy as jnp
from jax import lax
import numpy as np

D_MODEL = 2048
BATCH = 4
SEQ = 4096
DEPTH = 1
DEC_BATCH = 4
DEC_SEQ = 8192
PAST_LEN = 128

GRID_W = 64
HEAD_DIM = 128
N_Q_HEADS = 8
N_KV_HEADS = 2
Q_PER_KV = N_Q_HEADS // N_KV_HEADS
ATTN_WIDTH = N_Q_HEADS * HEAD_DIM
KV_WIDTH = N_KV_HEADS * HEAD_DIM
POOL_WINDOWS = (2, 4, 8, 16)
N_POOL_GROUPS = len(POOL_WINDOWS)
POOL_GROUP_WIDTH = 256
POOL_WIDTH = N_POOL_GROUPS * POOL_GROUP_WIDTH
MIX_WIDTH = ATTN_WIDTH + POOL_WIDTH
IN_WIDTH = ATTN_WIDTH + 2 * KV_WIDTH + POOL_WIDTH
D_FF = 5632
CONV_WIDTH = 3
ROPE_THETA = 10000.0
ROPE_AXIS_DIM = HEAD_DIM // 2
Q_BLOCK = 128
EPS = 1e-6

kernel_name = "hymba_style_gqa_pool_convglu_encoder"


def rmsnorm(x, g):
    xf = x.astype(jnp.float32)
    y = xf * lax.rsqrt(jnp.mean(xf * xf, axis=-1, keepdims=True) + EPS)
    return (y * g.astype(jnp.float32)).astype(x.dtype)


def axial_rope_tables(L):
    rows = L // GRID_W
    row = jnp.repeat(jnp.arange(rows, dtype=jnp.float32), GRID_W)
    col = jnp.tile(jnp.arange(GRID_W, dtype=jnp.float32), rows)
    inv_freq = ROPE_THETA ** (-jnp.arange(0, ROPE_AXIS_DIM, 2, dtype=jnp.float32) / ROPE_AXIS_DIM)
    ang = jnp.stack([row[:, None] * inv_freq, col[:, None] * inv_freq], axis=1)
    return jnp.cos(ang), jnp.sin(ang)


def apply_axial_rope(x, cos, sin):
    B, L, H, _ = x.shape
    xf = x.astype(jnp.float32).reshape(B, L, H, 2, ROPE_AXIS_DIM)
    x1, x2 = xf[..., : ROPE_AXIS_DIM // 2], xf[..., ROPE_AXIS_DIM // 2:]
    c, s = cos[None, :, None], sin[None, :, None]
    out = jnp.concatenate([x1 * c - x2 * s, x2 * c + x1 * s], axis=-1)
    return out.reshape(B, L, H, HEAD_DIM).astype(x.dtype)


def bidir_gqa(q, k, v):
    B, L, _, _ = q.shape
    nblk = L // Q_BLOCK
    qb = q.reshape(B, nblk, Q_BLOCK, N_KV_HEADS, Q_PER_KV, HEAD_DIM).transpose(1, 0, 2, 3, 4, 5)
    scale = HEAD_DIM ** -0.5

    def one_block(q_blk):
        s = jnp.einsum('bqkgd,bskd->bkgqs', q_blk, k, preferred_element_type=jnp.float32) * scale
        p = jax.nn.softmax(s, axis=-1)
        return jnp.einsum('bkgqs,bskd->bqkgd', p.astype(v.dtype), v)

    ob = lax.map(one_block, qb)
    return ob.transpose(1, 0, 2, 3, 4, 5).reshape(B, L, ATTN_WIDTH)


def multiscale_pool(u, w_pool, pool_scale):
    B, L, _ = u.shape
    uf = u.astype(jnp.float32).reshape(B, L, N_POOL_GROUPS, POOL_GROUP_WIDTH)
    csum = jnp.concatenate([jnp.zeros((B, 1, N_POOL_GROUPS, POOL_GROUP_WIDTH), jnp.float32),
                            jnp.cumsum(uf, axis=1)], axis=1)
    t = jnp.arange(L)
    outs = []
    for g, w in enumerate(POOL_WINDOWS):
        lo = jnp.clip(t - w // 2, 0, L)
        hi = jnp.clip(t + w // 2, 0, L)
        cnt = (hi - lo).astype(jnp.float32)
        cs = csum[:, :, g]
        mean = (cs[:, hi] - cs[:, lo]) / cnt[None, :, None]
        outs.append(mean - uf[:, :, g])
    d = jnp.stack(outs, axis=2).astype(u.dtype)
    y = jnp.einsum('blgc,gcd->blgd', d, w_pool)
    return y.reshape(B, L, POOL_WIDTH) * pool_scale


def dwconv_centred(h, w, b):
    hp = jnp.pad(h, ((0, 0), (1, 1), (0, 0)))
    return hp[:, :-2] * w[0] + hp[:, 1:-1] * w[1] + hp[:, 2:] * w[2] + b


def encoder_layer(x, g_norm_mix, w_in, g_q, g_k, w_pool, pool_scale, w_out,
                  g_norm_ffn, w_up, w_conv, b_conv, w_down):
    B, L, _ = x.shape
    h = rmsnorm(x, g_norm_mix)
    z = h @ w_in
    q = z[..., :ATTN_WIDTH].reshape(B, L, N_Q_HEADS, HEAD_DIM)
    k = z[..., ATTN_WIDTH:ATTN_WIDTH + KV_WIDTH].reshape(B, L, N_KV_HEADS, HEAD_DIM)
    v = z[..., ATTN_WIDTH + KV_WIDTH:ATTN_WIDTH + 2 * KV_WIDTH].reshape(B, L, N_KV_HEADS, HEAD_DIM)
    u = z[..., ATTN_WIDTH + 2 * KV_WIDTH:]
    cos, sin = axial_rope_tables(L)
    q = apply_axial_rope(rmsnorm(q, g_q), cos, sin)
    k = apply_axial_rope(rmsnorm(k, g_k), cos, sin)
    a = bidir_gqa(q, k, v)
    m = multiscale_pool(u, w_pool, pool_scale)
    x = x + jnp.concatenate([a, m], axis=-1) @ w_out
    h = rmsnorm(x, g_norm_ffn)
    gu = h @ w_up
    gate = dwconv_centred(gu[..., :D_FF], w_conv, b_conv)
    val = gu[..., D_FF:]
    return x + (jax.nn.silu(gate) * val) @ w_down


def run_trunk(x, g_norm_mix, w_in, g_q, g_k, w_pool, pool_scale, w_out,
              g_norm_ffn, w_up, w_conv, b_conv, w_down):
    for i in range(DEPTH):
        x = encoder_layer(x, g_norm_mix[i], w_in[i], g_q[i], g_k[i], w_pool[i], pool_scale[i],
                          w_out[i], g_norm_ffn[i], w_up[i], w_conv[i], b_conv[i], w_down[i])
    return x


def setup_inputs(seed: int = 0) -> dict:
    key = jax.random.key(seed)
    ks = jax.random.split(key, 16)
    f32 = jnp.float32
    nrm = lambda k, shape, s: jax.random.normal(k, shape, f32) * s
    return {
        "x_prompt": nrm(ks[0], (BATCH, SEQ, D_MODEL), 1.0),
        "x_sample": nrm(ks[1], (DEC_BATCH, DEC_SEQ, D_MODEL), 1.0),
        "g_norm_mix": 1.0 + nrm(ks[2], (DEPTH, D_MODEL), 0.02),
        "w_in": nrm(ks[3], (DEPTH, D_MODEL, IN_WIDTH), D_MODEL ** -0.5),
        "g_q": 1.0 + nrm(ks[4], (DEPTH, HEAD_DIM), 0.02),
        "g_k": 1.0 + nrm(ks[5], (DEPTH, HEAD_DIM), 0.02),
        "w_pool": nrm(ks[6], (DEPTH, N_POOL_GROUPS, POOL_GROUP_WIDTH, POOL_GROUP_WIDTH), POOL_GROUP_WIDTH ** -0.5),
        "pool_scale": 1.0 + nrm(ks[7], (DEPTH, POOL_WIDTH), 0.02),
        "w_out": nrm(ks[8], (DEPTH, MIX_WIDTH, D_MODEL), MIX_WIDTH ** -0.5),
        "g_norm_ffn": 1.0 + nrm(ks[9], (DEPTH, D_MODEL), 0.02),
        "w_up": nrm(ks[10], (DEPTH, D_MODEL, 2 * D_FF), D_MODEL ** -0.5),
        "w_conv": nrm(ks[11], (DEPTH, CONV_WIDTH, D_FF), CONV_WIDTH ** -0.5),
        "b_conv": nrm(ks[12], (DEPTH, D_FF), 0.01),
        "w_down": nrm(ks[13], (DEPTH, D_FF, D_MODEL), D_FF ** -0.5),
    }


def reference(x_prompt, x_sample, g_norm_mix, w_in, g_q, g_k, w_pool, pool_scale, w_out,
              g_norm_ffn, w_up, w_conv, b_conv, w_down):
    y_prompt = run_trunk(x_prompt, g_norm_mix, w_in, g_q, g_k, w_pool, pool_scale, w_out,
                         g_norm_ffn, w_up, w_conv, b_conv, w_down)
    y_sample = run_trunk(x_sample, g_norm_mix, w_in, g_q, g_k, w_pool, pool_scale, w_out,
                         g_norm_ffn, w_up, w_conv, b_conv, w_down)
    return (y_prompt, y_sample)
```

```python
import functools
import math

import jax
import jax.numpy as jnp
from jax import lax
from jax.experimental import pallas as pl
from jax.experimental.pallas import tpu as pltpu

D_MODEL = 2048
GRID_W = 64
HEAD_DIM = 128
N_Q_HEADS = 8
N_KV_HEADS = 2
Q_PER_KV = N_Q_HEADS // N_KV_HEADS
ATTN_WIDTH = N_Q_HEADS * HEAD_DIM
KV_WIDTH = N_KV_HEADS * HEAD_DIM
POOL_WINDOWS = (2, 4, 8, 16)
POOL_GROUP_WIDTH = 256
POOL_WIDTH = len(POOL_WINDOWS) * POOL_GROUP_WIDTH
MIX_WIDTH = ATTN_WIDTH + POOL_WIDTH
D_FF = 5632
ROPE_THETA = 10000.0
ROPE_AXIS_DIM = HEAD_DIM // 2
EPS = 1e-6

F32 = jnp.float32
BF16 = jnp.bfloat16

V7X_VMEM_BYTES = 64 * 1024 * 1024
SUBLANES_F32 = 8
SUBLANES_BF16 = 16
POOL_HALO = max(POOL_WINDOWS) // 2
assert POOL_HALO == SUBLANES_F32
CONV_HALO = SUBLANES_BF16

VMEM_LIMIT = 56 * 1024 * 1024


def _params(*sem):
    return pltpu.CompilerParams(dimension_semantics=sem, vmem_limit_bytes=VMEM_LIMIT)


def _rope_tables(L):
    t = jnp.arange(L)
    row = (t // GRID_W).astype(F32)
    col = (t % GRID_W).astype(F32)
    inv_freq = ROPE_THETA ** (-jnp.arange(0, ROPE_AXIS_DIM, 2, dtype=F32) / ROPE_AXIS_DIM)
    ang_r = row[:, None] * inv_freq
    ang_c = col[:, None] * inv_freq
    ang = jnp.concatenate([ang_r, ang_r, ang_c, ang_c], axis=1)
    cos, sin = jnp.cos(ang), jnp.sin(ang)
    first = (jnp.arange(HEAD_DIM) % ROPE_AXIS_DIM) < (ROPE_AXIS_DIM // 2)
    sin_up = jnp.where(first, -sin, 0.0)
    sin_dn = jnp.where(first, 0.0, sin)
    return cos, sin_up, sin_dn


def _proj_kernel(x_ref, gmix_ref, win_ref, gq_ref, gk_ref, cos_ref, sup_ref, sdn_ref,
                 qt_ref, k_ref, vt_ref, u_ref):
    x = x_ref[0]
    inv = lax.rsqrt(jnp.mean(x * x, axis=-1, keepdims=True) + EPS)
    h = (x * inv * gmix_ref[...]).astype(BF16)
    cos, sup, sdn = cos_ref[...], sup_ref[...], sdn_ref[...]

    def norm_rope(zh, g):
        r = lax.rsqrt(jnp.mean(zh * zh, axis=-1, keepdims=True) + EPS)
        n = zh * r * g
        up = pltpu.roll(n, HEAD_DIM - ROPE_AXIS_DIM // 2, 1)
        dn = pltpu.roll(n, ROPE_AXIS_DIM // 2, 1)
        return n * cos + up * sup + dn * sdn

    zq = jnp.dot(h, win_ref[:, 0:ATTN_WIDTH], preferred_element_type=F32)
    for hh in range(N_Q_HEADS):
        sl = slice(hh * HEAD_DIM, (hh + 1) * HEAD_DIM)
        qt_ref[0, sl, :] = norm_rope(zq[:, sl], gq_ref[...]).T.astype(BF16)
    zk = jnp.dot(h, win_ref[:, ATTN_WIDTH:ATTN_WIDTH + KV_WIDTH], preferred_element_type=F32)
    for hh in range(N_KV_HEADS):
        sl = slice(hh * HEAD_DIM, (hh + 1) * HEAD_DIM)
        k_ref[0, :, sl] = norm_rope(zk[:, sl], gk_ref[...]).astype(BF16)
    zv = jnp.dot(h, win_ref[:, ATTN_WIDTH + KV_WIDTH:ATTN_WIDTH + 2 * KV_WIDTH],
                 preferred_element_type=F32)
    for hh in range(N_KV_HEADS):
        sl = slice(hh * HEAD_DIM, (hh + 1) * HEAD_DIM)
        vt_ref[0, sl, :] = zv[:, sl].T.astype(BF16)
    u_ref[0] = jnp.dot(h, win_ref[:, ATTN_WIDTH + 2 * KV_WIDTH:], preferred_element_type=F32)


def _proj(x, g_mix, w_in, gq, gk, tables, tm):
    B, L, _ = x.shape
    cos, sup, sdn = tables
    tab_spec = pl.BlockSpec((tm, HEAD_DIM), lambda b, i: (i, 0))
    vec = lambda n: pl.BlockSpec((1, n), lambda b, i: (0, 0))
    return pl.pallas_call(
        _proj_kernel,
        grid=(B, L // tm),
        in_specs=[
            pl.BlockSpec((1, tm, D_MODEL), lambda b, i: (b, i, 0)),
            vec(D_MODEL),
            pl.BlockSpec(w_in.shape, lambda b, i: (0, 0)),
            vec(HEAD_DIM), vec(HEAD_DIM),
            tab_spec, tab_spec, tab_spec,
        ],
        out_specs=[
            pl.BlockSpec((1, ATTN_WIDTH, tm), lambda b, i: (b, 0, i)),
            pl.BlockSpec((1, tm, KV_WIDTH), lambda b, i: (b, i, 0)),
            pl.BlockSpec((1, KV_WIDTH, tm), lambda b, i: (b, 0, i)),
            pl.BlockSpec((1, tm, POOL_WIDTH), lambda b, i: (b, i, 0)),
        ],
        out_shape=[
            jax.ShapeDtypeStruct((B, ATTN_WIDTH, L), BF16),
            jax.ShapeDtypeStruct((B, L, KV_WIDTH), BF16),
            jax.ShapeDtypeStruct((B, KV_WIDTH, L), BF16),
            jax.ShapeDtypeStruct((B, L, POOL_WIDTH), F32),
        ],
        compiler_params=_params("parallel", "parallel"),
        name="proj",
    )(x, g_mix, w_in, gq, gk, cos, sup, sdn)


def _attn_kernel(qt_ref, k_ref, vt_ref, o_ref, acc_ref, *, tk):
    tq = qt_ref.shape[2]
    L = k_ref.shape[1]
    n = Q_PER_KV * tq
    qcat = jnp.concatenate(
        [qt_ref[0, g * HEAD_DIM:(g + 1) * HEAD_DIM, :] for g in range(Q_PER_KV)], axis=1)
    acc_ref[...] = jnp.zeros_like(acc_ref)

    def body(j, carry):
        m, l = carry
        start = pl.multiple_of(j * tk, tk)
        kc = k_ref[0, pl.ds(start, tk), :]
        st = jnp.dot(kc, qcat, preferred_element_type=F32)
        m_new = jnp.maximum(m, jnp.max(st, axis=0, keepdims=True))
        alpha = jnp.exp2(m - m_new)
        pt = jnp.exp2(st - m_new)
        l = alpha * l + jnp.sum(pt, axis=0, keepdims=True)
        vc = vt_ref[0, :, pl.ds(start, tk)]
        acc_ref[...] = alpha * acc_ref[...] + jnp.dot(
            vc, pt.astype(BF16), preferred_element_type=F32)
        return m_new, l

    m0 = jnp.full((1, n), -jnp.inf, F32)
    l0 = jnp.zeros((1, n), F32)
    _, l = lax.fori_loop(0, L // tk, body, (m0, l0))
    out_t = acc_ref[...] / l
    for g in range(Q_PER_KV):
        o_ref[0, :, g * HEAD_DIM:(g + 1) * HEAD_DIM] = (
            out_t[:, g * tq:(g + 1) * tq].T.astype(BF16))


def _attention(qt, k, vt, tq, tk):
    B, _, L = qt.shape
    gw = Q_PER_KV * HEAD_DIM
    return pl.pallas_call(
        functools.partial(_attn_kernel, tk=tk),
        grid=(B, N_KV_HEADS, L // tq),
        in_specs=[
            pl.BlockSpec((1, gw, tq), lambda b, kh, i: (b, kh, i)),
            pl.BlockSpec((1, L, HEAD_DIM), lambda b, kh, i: (b, 0, kh)),
            pl.BlockSpec((1, HEAD_DIM, L), lambda b, kh, i: (b, kh, 0)),
        ],
        out_specs=pl.BlockSpec((1, tq, gw), lambda b, kh, i: (b, i, kh)),
        out_shape=jax.ShapeDtypeStruct((B, L, ATTN_WIDTH), BF16),
        scratch_shapes=[pltpu.VMEM((HEAD_DIM, Q_PER_KV * tq), F32)],
        compiler_params=_params("parallel", "parallel", "parallel"),
        name="attention",
    )(qt, k, vt)


def _mix_kernel(a_ref, u_ref, uprev_ref, unext_ref, x_ref, wpool_ref, pscale_ref, wout_ref,
                gffn_ref, x1_ref, h2_ref, uext_ref, cat_ref, *, seq_len):
    i = pl.program_id(1)
    last = pl.num_programs(1) - 1
    tm = u_ref.shape[1]
    hal = POOL_HALO
    uext_ref[0:hal, :] = jnp.where(i > 0, uprev_ref[0], 0.0)
    uext_ref[hal:hal + tm, :] = u_ref[0]
    uext_ref[hal + tm:, :] = jnp.where(i < last, unext_ref[0], 0.0)

    t = i * tm + lax.broadcasted_iota(jnp.int32, (tm, 1), 0)
    cat_ref[:, 0:ATTN_WIDTH] = a_ref[0]
    for g, w in enumerate(POOL_WINDOWS):
        c0 = g * POOL_GROUP_WIDTH
        cols = slice(c0, c0 + POOL_GROUP_WIDTH)
        tot = uext_ref[hal - w // 2:hal - w // 2 + tm, cols]
        for d in range(-w // 2 + 1, w // 2):
            tot = tot + uext_ref[hal + d:hal + d + tm, cols]
        cnt = (jnp.minimum(t + w // 2, seq_len) - jnp.maximum(t - w // 2, 0)).astype(F32)
        dm = tot / cnt - uext_ref[hal:hal + tm, cols]
        y = jnp.dot(dm.astype(BF16), wpool_ref[g], preferred_element_type=F32)
        cat_ref[:, ATTN_WIDTH + c0:ATTN_WIDTH + c0 + POOL_GROUP_WIDTH] = (
            y * pscale_ref[:, cols]).astype(BF16)

    x1 = x_ref[0] + jnp.dot(cat_ref[...], wout_ref[...], preferred_element_type=F32)
    x1_ref[0] = x1
    inv = lax.rsqrt(jnp.mean(x1 * x1, axis=-1, keepdims=True) + EPS)
    h2_ref[0] = (x1 * inv * gffn_ref[...]).astype(BF16)


def _mix(a, u, x, w_pool, pool_scale, w_out, g_ffn, tm):
    B, L, _ = x.shape
    nb = tm // POOL_HALO
    nblk = L // POOL_HALO
    return pl.pallas_call(
        functools.partial(_mix_kernel, seq_len=L),
        grid=(B, L // tm),
        in_specs=[
            pl.BlockSpec((1, tm, ATTN_WIDTH), lambda b, i: (b, i, 0)),
            pl.BlockSpec((1, tm, POOL_WIDTH), lambda b, i: (b, i, 0)),
            pl.BlockSpec((1, POOL_HALO, POOL_WIDTH),
                         lambda b, i: (b, jnp.maximum(i * nb - 1, 0), 0)),
            pl.BlockSpec((1, POOL_HALO, POOL_WIDTH),
                         lambda b, i: (b, jnp.minimum((i + 1) * nb, nblk - 1), 0)),
            pl.BlockSpec((1, tm, D_MODEL), lambda b, i: (b, i, 0)),
            pl.BlockSpec(w_pool.shape, lambda b, i: (0, 0, 0)),
            pl.BlockSpec((1, POOL_WIDTH), lambda b, i: (0, 0)),
            pl.BlockSpec(w_out.shape, lambda b, i: (0, 0)),
            pl.BlockSpec((1, D_MODEL), lambda b, i: (0, 0)),
        ],
        out_specs=[
            pl.BlockSpec((1, tm, D_MODEL), lambda b, i: (b, i, 0)),
            pl.BlockSpec((1, tm, D_MODEL), lambda b, i: (b, i, 0)),
        ],
        out_shape=[
            jax.ShapeDtypeStruct((B, L, D_MODEL), F32),
            jax.ShapeDtypeStruct((B, L, D_MODEL), BF16),
        ],
        scratch_shapes=[
            pltpu.VMEM((tm + 2 * POOL_HALO, POOL_WIDTH), F32),
            pltpu.VMEM((tm, MIX_WIDTH), BF16),
        ],
        compiler_params=_params("parallel", "parallel"),
        name="mix_out",
    )(a, u, u, u, x, w_pool, pool_scale, w_out, g_ffn)


def _ffn_kernel(h_ref, hprev_ref, hnext_ref, x1_ref, wg_ref, wv_ref, wc_ref, bc_ref, wd_ref,
                o_ref, hext_ref, acc_ref):
    i = pl.program_id(1)
    j = pl.program_id(2)
    last_i = pl.num_programs(1) - 1
    last_j = pl.num_programs(2) - 1
    tm = h_ref.shape[1]
    hal = CONV_HALO

    @pl.when(j == 0)
    def _():
        hext_ref[0:hal, :] = jnp.where(i > 0, hprev_ref[0], jnp.zeros_like(hprev_ref[0]))
        hext_ref[hal:hal + tm, :] = h_ref[0]
        hext_ref[hal + tm:, :] = jnp.where(i < last_i, hnext_ref[0], jnp.zeros_like(hnext_ref[0]))
        acc_ref[...] = jnp.zeros_like(acc_ref)

    ge = jnp.dot(hext_ref[...], wg_ref[...], preferred_element_type=F32)
    val = jnp.dot(h_ref[0], wv_ref[...], preferred_element_type=F32)
    gate = (ge[hal - 1:hal - 1 + tm] * wc_ref[0:1, :] + ge[hal:hal + tm] * wc_ref[1:2, :]
            + ge[hal + 1:hal + 1 + tm] * wc_ref[2:3, :] + bc_ref[...])
    act = gate * (1.0 / (1.0 + jnp.exp(-gate))) * val
    acc_ref[...] += jnp.dot(act.astype(BF16), wd_ref[...], preferred_element_type=F32)

    @pl.when(j == last_j)
    def _():
        o_ref[0] = x1_ref[0] + acc_ref[...]


def _ffn(h2, x1, w_up, w_conv, b_conv, w_down, tm, tf):
    B, L, _ = x1.shape
    nb = tm // CONV_HALO
    nblk = L // CONV_HALO
    nf = D_FF // tf
    return pl.pallas_call(
        _ffn_kernel,
        grid=(B, L // tm, nf),
        in_specs=[
            pl.BlockSpec((1, tm, D_MODEL), lambda b, i, j: (b, i, 0)),
            pl.BlockSpec((1, CONV_HALO, D_MODEL),
                         lambda b, i, j: (b, jnp.maximum(i * nb - 1, 0), 0)),
            pl.BlockSpec((1, CONV_HALO, D_MODEL),
                         lambda b, i, j: (b, jnp.minimum((i + 1) * nb, nblk - 1), 0)),
            pl.BlockSpec((1, tm, D_MODEL), lambda b, i, j: (b, i, 0)),
            pl.BlockSpec((D_MODEL, tf), lambda b, i, j: (0, j)),
            pl.BlockSpec((D_MODEL, tf), lambda b, i, j: (0, j + nf)),
            pl.BlockSpec((3, tf), lambda b, i, j: (0, j)),
            pl.BlockSpec((1, tf), lambda b, i, j: (0, j)),
            pl.BlockSpec((tf, D_MODEL), lambda b, i, j: (j, 0)),
        ],
        out_specs=pl.BlockSpec((1, tm, D_MODEL), lambda b, i, j: (b, i, 0)),
        out_shape=jax.ShapeDtypeStruct((B, L, D_MODEL), F32),
        scratch_shapes=[
            pltpu.VMEM((tm + 2 * CONV_HALO, D_MODEL), BF16),
            pltpu.VMEM((tm, D_MODEL), F32),
        ],
        compiler_params=_params("parallel", "parallel", "arbitrary"),
        name="ffn",
    )(h2, h2, h2, x1, w_up, w_up, w_conv, b_conv, w_down)


def _tiles(L):
    tm = min(512, L)
    tq = min(256, L)
    tk = min(512, L)
    tf = 512
    return tm, tq, tk, tf


def _layer(x, p):
    B, L, _ = x.shape
    tm, tq, tk, tf = _tiles(L)
    tables = _rope_tables(L)
    qt, k, vt, u = _proj(x, p["g_mix"], p["w_in"], p["gq"], p["gk"], tables, tm)
    a = _attention(qt, k, vt, tq, tk)
    x1, h2 = _mix(a, u, x, p["w_pool"], p["pool_scale"], p["w_out"], p["g_ffn"], tm)
    return _ffn(h2, x1, p["w_up"], p["w_conv"], p["b_conv"], p["w_down"], tm, tf)


def _prep(i, g_norm_mix, w_in, g_q, g_k, w_pool, pool_scale, w_out, g_norm_ffn, w_up, w_conv,
          b_conv, w_down):
    q_scale = HEAD_DIM ** -0.5 * math.log2(math.e)
    return dict(
        g_mix=g_norm_mix[i][None, :],
        w_in=w_in[i].astype(BF16),
        gq=(g_q[i] * q_scale)[None, :],
        gk=g_k[i][None, :],
        w_pool=w_pool[i].astype(BF16),
        pool_scale=pool_scale[i][None, :],
        w_out=w_out[i].astype(BF16),
        g_ffn=g_norm_ffn[i][None, :],
        w_up=w_up[i].astype(BF16),
        w_conv=w_conv[i],
        b_conv=b_conv[i][None, :],
        w_down=w_down[i].astype(BF16),
    )


def kernel(x_prompt, x_sample, g_norm_mix, w_in, g_q, g_k, w_pool, pool_scale, w_out,
           g_norm_ffn, w_up, w_conv, b_conv, w_down):
    depth = w_in.shape[0]
    layers = [_prep(i, g_norm_mix, w_in, g_q, g_k, w_pool, pool_scale, w_out, g_norm_ffn,
                    w_up, w_conv, b_conv, w_down) for i in range(depth)]
    outs = []
    for x in (x_prompt, x_sample):
        for p in layers:
            x = _layer(x, p)
        outs.append(x)
    return tuple(outs)
```

```python
import functools
import math

import jax
import jax.numpy as jnp
from jax import lax
from jax.experimental import pallas as pl
from jax.experimental.pallas import tpu as pltpu

D_MODEL = 2048
GRID_W = 64
HEAD_DIM = 128
N_Q_HEADS = 8
N_KV_HEADS = 2
Q_PER_KV = N_Q_HEADS // N_KV_HEADS
ATTN_WIDTH = N_Q_HEADS * HEAD_DIM
KV_WIDTH = N_KV_HEADS * HEAD_DIM
POOL_WINDOWS = (2, 4, 8, 16)
POOL_GROUP_WIDTH = 256
POOL_WIDTH = len(POOL_WINDOWS) * POOL_GROUP_WIDTH
MIX_WIDTH = ATTN_WIDTH + POOL_WIDTH
D_FF = 5632
ROPE_THETA = 10000.0
ROPE_AXIS_DIM = HEAD_DIM // 2
EPS = 1e-6

F32 = jnp.float32
BF16 = jnp.bfloat16

V7X_VMEM_BYTES = 64 * 1024 * 1024
SUBLANES_F32 = 8
SUBLANES_BF16 = 16
POOL_HALO = max(POOL_WINDOWS) // 2
assert POOL_HALO == SUBLANES_F32
CONV_HALO = SUBLANES_BF16

VMEM_LIMIT = 56 * 1024 * 1024


def _params(*sem):
    return pltpu.CompilerParams(dimension_semantics=sem, vmem_limit_bytes=VMEM_LIMIT)


def _rope_tables(L):
    t = jnp.arange(L)
    row = (t // GRID_W).astype(F32)
    col = (t % GRID_W).astype(F32)
    inv_freq = ROPE_THETA ** (-jnp.arange(0, ROPE_AXIS_DIM, 2, dtype=F32) / ROPE_AXIS_DIM)
    ang_r = row[:, None] * inv_freq
    ang_c = col[:, None] * inv_freq
    ang = jnp.concatenate([ang_r, ang_r, ang_c, ang_c], axis=1)
    cos, sin = jnp.cos(ang), jnp.sin(ang)
    first = (jnp.arange(HEAD_DIM) % ROPE_AXIS_DIM) < (ROPE_AXIS_DIM // 2)
    sin_up = jnp.where(first, -sin, 0.0)
    sin_dn = jnp.where(first, 0.0, sin)
    return cos, sin_up, sin_dn


def _proj_kernel(x_ref, gmix_ref, win_ref, gq_ref, gk_ref, cos_ref, sup_ref, sdn_ref,
                 qt_ref, k_ref, vt_ref, u_ref):
    x = x_ref[0]
    inv = lax.rsqrt(jnp.mean(x * x, axis=-1, keepdims=True) + EPS)
    h = (x * inv * gmix_ref[...]).astype(BF16)
    cos, sup, sdn = cos_ref[...], sup_ref[...], sdn_ref[...]

    def norm_rope(zh, g):
        r = lax.rsqrt(jnp.mean(zh * zh, axis=-1, keepdims=True) + EPS)
        n = zh * r * g
        up = pltpu.roll(n, HEAD_DIM - ROPE_AXIS_DIM // 2, 1)
        dn = pltpu.roll(n, ROPE_AXIS_DIM // 2, 1)
        return n * cos + up * sup + dn * sdn

    zq = jnp.dot(h, win_ref[:, 0:ATTN_WIDTH], preferred_element_type=F32)
    for hh in range(N_Q_HEADS):
        sl = slice(hh * HEAD_DIM, (hh + 1) * HEAD_DIM)
        qt_ref[0, sl, :] = norm_rope(zq[:, sl], gq_ref[...]).T.astype(BF16)
    zk = jnp.dot(h, win_ref[:, ATTN_WIDTH:ATTN_WIDTH + KV_WIDTH], preferred_element_type=F32)
    for hh in range(N_KV_HEADS):
        sl = slice(hh * HEAD_DIM, (hh + 1) * HEAD_DIM)
        k_ref[0, :, sl] = norm_rope(zk[:, sl], gk_ref[...]).astype(BF16)
    zv = jnp.dot(h, win_ref[:, ATTN_WIDTH + KV_WIDTH:ATTN_WIDTH + 2 * KV_WIDTH],
                 preferred_element_type=F32)
    for hh in range(N_KV_HEADS):
        sl = slice(hh * HEAD_DIM, (hh + 1) * HEAD_DIM)
        vt_ref[0, sl, :] = zv[:, sl].T.astype(BF16)
    u_ref[0] = jnp.dot(h, win_ref[:, ATTN_WIDTH + 2 * KV_WIDTH:], preferred_element_type=F32)


def _proj(x, g_mix, w_in, gq, gk, tables, tm):
    B, L, _ = x.shape
    cos, sup, sdn = tables
    tab_spec = pl.BlockSpec((tm, HEAD_DIM), lambda b, i: (i, 0))
    vec = lambda n: pl.BlockSpec((1, n), lambda b, i: (0, 0))
    return pl.pallas_call(
        _proj_kernel,
        grid=(B, L // tm),
        in_specs=[
            pl.BlockSpec((1, tm, D_MODEL), lambda b, i: (b, i, 0)),
            vec(D_MODEL),
            pl.BlockSpec(w_in.shape, lambda b, i: (0, 0)),
            vec(HEAD_DIM), vec(HEAD_DIM),
            tab_spec, tab_spec, tab_spec,
        ],
        out_specs=[
            pl.BlockSpec((1, ATTN_WIDTH, tm), lambda b, i: (b, 0, i)),
            pl.BlockSpec((1, tm, KV_WIDTH), lambda b, i: (b, i, 0)),
            pl.BlockSpec((1, KV_WIDTH, tm), lambda b, i: (b, 0, i)),
            pl.BlockSpec((1, tm, POOL_WIDTH), lambda b, i: (b, i, 0)),
        ],
        out_shape=[
            jax.ShapeDtypeStruct((B, ATTN_WIDTH, L), BF16),
            jax.ShapeDtypeStruct((B, L, KV_WIDTH), BF16),
            jax.ShapeDtypeStruct((B, KV_WIDTH, L), BF16),
            jax.ShapeDtypeStruct((B, L, POOL_WIDTH), F32),
        ],
        compiler_params=_params("parallel", "parallel"),
        name="proj",
    )(x, g_mix, w_in, gq, gk, cos, sup, sdn)


def _attn_kernel(qt_ref, k_ref, vt_ref, o_ref, acc_ref, s_ref, *, tq, tk, unroll):
    L = k_ref.shape[1]
    n = Q_PER_KV * tq
    nk = L // tk
    nq = L // tq
    assert nk % unroll == 0 and unroll % 2 == 0
    ones_rows = (lax.broadcasted_iota(jnp.int32, (SUBLANES_BF16, tk), 0) == 0).astype(BF16)

    def qcat_at(qi):
        c0 = pl.multiple_of(qi * tq, tq)
        return jnp.concatenate(
            [qt_ref[0, g * HEAD_DIM:(g + 1) * HEAD_DIM, pl.ds(c0, tq)] for g in range(Q_PER_KV)],
            axis=1)

    def scores(qc, j):
        start = pl.multiple_of(j * tk, tk)
        return jnp.dot(k_ref[0, pl.ds(start, tk), :], qc,
                       preferred_element_type=F32)

    def step(j, slot, m, next_q, next_j):
        s_ref[1 - slot] = scores(next_q, next_j)
        st = s_ref[slot]
        m_new = jnp.maximum(m, jnp.max(st, axis=0, keepdims=True))
        alpha = jnp.exp2(m - m_new)
        pt = jnp.exp2(st - m_new).astype(BF16)
        start = pl.multiple_of(j * tk, tk)
        vc = jnp.concatenate([vt_ref[0, :, pl.ds(start, tk)], ones_rows], axis=0)
        acc_ref[...] = alpha * acc_ref[...] + jnp.dot(vc, pt, preferred_element_type=F32)
        return m_new

    s_ref[0] = scores(qcat_at(0), 0)

    def tile(qi, carry):
        qc = qcat_at(qi)
        acc_ref[...] = jnp.zeros_like(acc_ref)

        def group(jj, m):
            for si in range(unroll):
                j = jj * unroll + si
                m = step(j, si % 2, m, qc, j + 1)
            return m

        m = lax.fori_loop(0, nk // unroll - 1, group, jnp.full((1, n), -jnp.inf, F32))
        for si in range(unroll - 1):
            j = nk - unroll + si
            m = step(j, si % 2, m, qc, j + 1)
        step(nk - 1, 1, m, qcat_at(jnp.minimum(qi + 1, nq - 1)), 0)
        out_t = acc_ref[0:HEAD_DIM, :] / acc_ref[HEAD_DIM:HEAD_DIM + 1, :]
        r0 = pl.multiple_of(qi * tq, tq)
        for g in range(Q_PER_KV):
            o_ref[0, pl.ds(r0, tq), g * HEAD_DIM:(g + 1) * HEAD_DIM] = (
                out_t[:, g * tq:(g + 1) * tq].T.astype(BF16))
        return carry

    lax.fori_loop(0, nq, tile, 0)


def _attention(qt, k, vt, tq, tk, unroll):
    B, _, L = qt.shape
    gw = Q_PER_KV * HEAD_DIM
    return pl.pallas_call(
        functools.partial(_attn_kernel, tq=tq, tk=tk, unroll=unroll),
        grid=(B, N_KV_HEADS),
        in_specs=[
            pl.BlockSpec((1, gw, L), lambda b, kh: (b, kh, 0)),
            pl.BlockSpec((1, L, HEAD_DIM), lambda b, kh: (b, 0, kh)),
            pl.BlockSpec((1, HEAD_DIM, L), lambda b, kh: (b, kh, 0)),
        ],
        out_specs=pl.BlockSpec((1, L, gw), lambda b, kh: (b, 0, kh)),
        out_shape=jax.ShapeDtypeStruct((B, L, ATTN_WIDTH), BF16),
        scratch_shapes=[
            pltpu.VMEM((HEAD_DIM + SUBLANES_BF16, Q_PER_KV * tq), F32),
            pltpu.VMEM((2, tk, Q_PER_KV * tq), F32),
        ],
        compiler_params=_params("parallel", "parallel"),
        name="attention",
    )(qt, k, vt)


def _mix_kernel(a_ref, u_ref, uprev_ref, unext_ref, x_ref, wpool_ref, pscale_ref, wout_ref,
                gffn_ref, x1_ref, h2_ref, uext_ref, cat_ref, *, seq_len):
    i = pl.program_id(1)
    last = pl.num_programs(1) - 1
    tm = u_ref.shape[1]
    hal = POOL_HALO
    uext_ref[0:hal, :] = jnp.where(i > 0, uprev_ref[0], 0.0)
    uext_ref[hal:hal + tm, :] = u_ref[0]
    uext_ref[hal + tm:, :] = jnp.where(i < last, unext_ref[0], 0.0)

    t = i * tm + lax.broadcasted_iota(jnp.int32, (tm, 1), 0)
    cat_ref[:, 0:ATTN_WIDTH] = a_ref[0]
    for g, w in enumerate(POOL_WINDOWS):
        c0 = g * POOL_GROUP_WIDTH
        cols = slice(c0, c0 + POOL_GROUP_WIDTH)
        tot = uext_ref[hal - w // 2:hal - w // 2 + tm, cols]
        for d in range(-w // 2 + 1, w // 2):
            tot = tot + uext_ref[hal + d:hal + d + tm, cols]
        cnt = (jnp.minimum(t + w // 2, seq_len) - jnp.maximum(t - w // 2, 0)).astype(F32)
        dm = tot / cnt - uext_ref[hal:hal + tm, cols]
        y = jnp.dot(dm.astype(BF16), wpool_ref[g], preferred_element_type=F32)
        cat_ref[:, ATTN_WIDTH + c0:ATTN_WIDTH + c0 + POOL_GROUP_WIDTH] = (
            y * pscale_ref[:, cols]).astype(BF16)

    x1 = x_ref[0] + jnp.dot(cat_ref[...], wout_ref[...], preferred_element_type=F32)
    x1_ref[0] = x1
    inv = lax.rsqrt(jnp.mean(x1 * x1, axis=-1, keepdims=True) + EPS)
    h2_ref[0] = (x1 * inv * gffn_ref[...]).astype(BF16)


def _mix(a, u, x, w_pool, pool_scale, w_out, g_ffn, tm):
    B, L, _ = x.shape
    nb = tm // POOL_HALO
    nblk = L // POOL_HALO
    return pl.pallas_call(
        functools.partial(_mix_kernel, seq_len=L),
        grid=(B, L // tm),
        in_specs=[
            pl.BlockSpec((1, tm, ATTN_WIDTH), lambda b, i: (b, i, 0)),
            pl.BlockSpec((1, tm, POOL_WIDTH), lambda b, i: (b, i, 0)),
            pl.BlockSpec((1, POOL_HALO, POOL_WIDTH),
                         lambda b, i: (b, jnp.maximum(i * nb - 1, 0), 0)),
            pl.BlockSpec((1, POOL_HALO, POOL_WIDTH),
                         lambda b, i: (b, jnp.minimum((i + 1) * nb, nblk - 1), 0)),
            pl.BlockSpec((1, tm, D_MODEL), lambda b, i: (b, i, 0)),
            pl.BlockSpec(w_pool.shape, lambda b, i: (0, 0, 0)),
            pl.BlockSpec((1, POOL_WIDTH), lambda b, i: (0, 0)),
            pl.BlockSpec(w_out.shape, lambda b, i: (0, 0)),
            pl.BlockSpec((1, D_MODEL), lambda b, i: (0, 0)),
        ],
        out_specs=[
            pl.BlockSpec((1, tm, D_MODEL), lambda b, i: (b, i, 0)),
            pl.BlockSpec((1, tm, D_MODEL), lambda b, i: (b, i, 0)),
        ],
        out_shape=[
            jax.ShapeDtypeStruct((B, L, D_MODEL), F32),
            jax.ShapeDtypeStruct((B, L, D_MODEL), BF16),
        ],
        scratch_shapes=[
            pltpu.VMEM((tm + 2 * POOL_HALO, POOL_WIDTH), F32),
            pltpu.VMEM((tm, MIX_WIDTH), BF16),
        ],
        compiler_params=_params("parallel", "parallel"),
        name="mix_out",
    )(a, u, u, u, x, w_pool, pool_scale, w_out, g_ffn)


def _ffn_kernel(h_ref, hprev_ref, hnext_ref, x1_ref, wg_ref, wv_ref, wc_ref, bc_ref, wd_ref,
                o_ref, hext_ref, acc_ref):
    i = pl.program_id(1)
    j = pl.program_id(2)
    last_i = pl.num_programs(1) - 1
    last_j = pl.num_programs(2) - 1
    tm = h_ref.shape[1]
    hal = CONV_HALO

    @pl.when(j == 0)
    def _():
        hext_ref[0:hal, :] = jnp.where(i > 0, hprev_ref[0], jnp.zeros_like(hprev_ref[0]))
        hext_ref[hal:hal + tm, :] = h_ref[0]
        hext_ref[hal + tm:, :] = jnp.where(i < last_i, hnext_ref[0], jnp.zeros_like(hnext_ref[0]))
        acc_ref[...] = jnp.zeros_like(acc_ref)

    ge = jnp.dot(hext_ref[...], wg_ref[...], preferred_element_type=F32)
    val = jnp.dot(h_ref[0], wv_ref[...], preferred_element_type=F32)
    gate = (ge[hal - 1:hal - 1 + tm] * wc_ref[0:1, :] + ge[hal:hal + tm] * wc_ref[1:2, :]
            + ge[hal + 1:hal + 1 + tm] * wc_ref[2:3, :] + bc_ref[...])
    act = gate * (1.0 / (1.0 + jnp.exp(-gate))) * val
    acc_ref[...] += jnp.dot(act.astype(BF16), wd_ref[...], preferred_element_type=F32)

    @pl.when(j == last_j)
    def _():
        o_ref[0] = x1_ref[0] + acc_ref[...]


def _ffn(h2, x1, w_up, w_conv, b_conv, w_down, tm, tf):
    B, L, _ = x1.shape
    nb = tm // CONV_HALO
    nblk = L // CONV_HALO
    nf = D_FF // tf
    return pl.pallas_call(
        _ffn_kernel,
        grid=(B, L // tm, nf),
        in_specs=[
            pl.BlockSpec((1, tm, D_MODEL), lambda b, i, j: (b, i, 0)),
            pl.BlockSpec((1, CONV_HALO, D_MODEL),
                         lambda b, i, j: (b, jnp.maximum(i * nb - 1, 0), 0)),
            pl.BlockSpec((1, CONV_HALO, D_MODEL),
                         lambda b, i, j: (b, jnp.minimum((i + 1) * nb, nblk - 1), 0)),
            pl.BlockSpec((1, tm, D_MODEL), lambda b, i, j: (b, i, 0)),
            pl.BlockSpec((D_MODEL, tf), lambda b, i, j: (0, j)),
            pl.BlockSpec((D_MODEL, tf), lambda b, i, j: (0, j + nf)),
            pl.BlockSpec((3, tf), lambda b, i, j: (0, j)),
            pl.BlockSpec((1, tf), lambda b, i, j: (0, j)),
            pl.BlockSpec((tf, D_MODEL), lambda b, i, j: (j, 0)),
        ],
        out_specs=pl.BlockSpec((1, tm, D_MODEL), lambda b, i, j: (b, i, 0)),
        out_shape=jax.ShapeDtypeStruct((B, L, D_MODEL), F32),
        scratch_shapes=[
            pltpu.VMEM((tm + 2 * CONV_HALO, D_MODEL), BF16),
            pltpu.VMEM((tm, D_MODEL), F32),
        ],
        compiler_params=_params("parallel", "parallel", "arbitrary"),
        name="ffn",
    )(h2, h2, h2, x1, w_up, w_up, w_conv, b_conv, w_down)


def _tiles(L):
    tm = min(512, L)
    tq = min(256, L)
    tk = min(512, L)
    tf = 512
    attn_unroll = 4
    return tm, tq, tk, tf, attn_unroll


def _layer(x, p):
    B, L, _ = x.shape
    tm, tq, tk, tf, attn_unroll = _tiles(L)
    tables = _rope_tables(L)
    qt, k, vt, u = _proj(x, p["g_mix"], p["w_in"], p["gq"], p["gk"], tables, tm)
    a = _attention(qt, k, vt, tq, tk, attn_unroll)
    x1, h2 = _mix(a, u, x, p["w_pool"], p["pool_scale"], p["w_out"], p["g_ffn"], tm)
    return _ffn(h2, x1, p["w_up"], p["w_conv"], p["b_conv"], p["w_down"], tm, tf)


def _prep(i, g_norm_mix, w_in, g_q, g_k, w_pool, pool_scale, w_out, g_norm_ffn, w_up, w_conv,
          b_conv, w_down):
    q_scale = HEAD_DIM ** -0.5 * math.log2(math.e)
    return dict(
        g_mix=g_norm_mix[i][None, :],
        w_in=w_in[i].astype(BF16),
        gq=(g_q[i] * q_scale)[None, :],
        gk=g_k[i][None, :],
        w_pool=w_pool[i].astype(BF16),
        pool_scale=pool_scale[i][None, :],
        w_out=w_out[i].astype(BF16),
        g_ffn=g_norm_ffn[i][None, :],
        w_up=w_up[i].astype(BF16),
        w_conv=w_conv[i],
        b_conv=b_conv[i][None, :],
        w_down=w_down[i].astype(BF16),
    )


def kernel(x_prompt, x_sample, g_norm_mix, w_in, g_q, g_k, w_pool, pool_scale, w_out,
           g_norm_ffn, w_up, w_conv, b_conv, w_down):
    depth = w_in.shape[0]
    layers = [_prep(i, g_norm_mix, w_in, g_q, g_k, w_pool, pool_scale, w_out, g_norm_ffn,
                    w_up, w_conv, b_conv, w_down) for i in range(depth)]
    outs = []
    for x in (x_prompt, x_sample):
        for p in layers:
            x = _layer(x, p)
        outs.append(x)
    return tuple(outs)
```

```python
import functools
import math

import jax
import jax.numpy as jnp
from jax import lax
from jax.experimental import pallas as pl
from jax.experimental.pallas import tpu as pltpu

D_MODEL = 2048
GRID_W = 64
HEAD_DIM = 128
N_Q_HEADS = 8
N_KV_HEADS = 2
Q_PER_KV = N_Q_HEADS // N_KV_HEADS
ATTN_WIDTH = N_Q_HEADS * HEAD_DIM
KV_WIDTH = N_KV_HEADS * HEAD_DIM
POOL_WINDOWS = (2, 4, 8, 16)
POOL_GROUP_WIDTH = 256
POOL_WIDTH = len(POOL_WINDOWS) * POOL_GROUP_WIDTH
MIX_WIDTH = ATTN_WIDTH + POOL_WIDTH
D_FF = 5632
ROPE_THETA = 10000.0
ROPE_AXIS_DIM = HEAD_DIM // 2
EPS = 1e-6

F32 = jnp.float32
BF16 = jnp.bfloat16

V7X_VMEM_BYTES = 64 * 1024 * 1024
SUBLANES_F32 = 8
SUBLANES_BF16 = 16
POOL_HALO = max(POOL_WINDOWS) // 2
assert POOL_HALO == SUBLANES_F32
CONV_HALO = SUBLANES_BF16

VMEM_LIMIT = 56 * 1024 * 1024


def _params(*sem):
    return pltpu.CompilerParams(dimension_semantics=sem, vmem_limit_bytes=VMEM_LIMIT)


def _rope_tables(L):
    t = jnp.arange(L)
    row = (t // GRID_W).astype(F32)
    col = (t % GRID_W).astype(F32)
    inv_freq = ROPE_THETA ** (-jnp.arange(0, ROPE_AXIS_DIM, 2, dtype=F32) / ROPE_AXIS_DIM)
    ang_r = row[:, None] * inv_freq
    ang_c = col[:, None] * inv_freq
    ang = jnp.concatenate([ang_r, ang_r, ang_c, ang_c], axis=1)
    cos, sin = jnp.cos(ang), jnp.sin(ang)
    first = (jnp.arange(HEAD_DIM) % ROPE_AXIS_DIM) < (ROPE_AXIS_DIM // 2)
    sin_up = jnp.where(first, -sin, 0.0)
    sin_dn = jnp.where(first, 0.0, sin)
    return cos, sin_up, sin_dn


def _proj_kernel(x_ref, gmix_ref, win_ref, gq_ref, gk_ref, cos_ref, sup_ref, sdn_ref,
                 qt_ref, k_ref, vt_ref, u_ref):
    x = x_ref[0]
    inv = lax.rsqrt(jnp.mean(x * x, axis=-1, keepdims=True) + EPS)
    h = (x * inv * gmix_ref[...]).astype(BF16)
    cos, sup, sdn = cos_ref[...], sup_ref[...], sdn_ref[...]

    def norm_rope(zh, g):
        r = lax.rsqrt(jnp.mean(zh * zh, axis=-1, keepdims=True) + EPS)
        n = zh * r * g
        up = pltpu.roll(n, HEAD_DIM - ROPE_AXIS_DIM // 2, 1)
        dn = pltpu.roll(n, ROPE_AXIS_DIM // 2, 1)
        return n * cos + up * sup + dn * sdn

    zq = jnp.dot(h, win_ref[:, 0:ATTN_WIDTH], preferred_element_type=F32)
    for hh in range(N_Q_HEADS):
        sl = slice(hh * HEAD_DIM, (hh + 1) * HEAD_DIM)
        qt_ref[0, sl, :] = norm_rope(zq[:, sl], gq_ref[...]).T.astype(BF16)
    zk = jnp.dot(h, win_ref[:, ATTN_WIDTH:ATTN_WIDTH + KV_WIDTH], preferred_element_type=F32)
    for hh in range(N_KV_HEADS):
        sl = slice(hh * HEAD_DIM, (hh + 1) * HEAD_DIM)
        k_ref[0, :, sl] = norm_rope(zk[:, sl], gk_ref[...]).astype(BF16)
    zv = jnp.dot(h, win_ref[:, ATTN_WIDTH + KV_WIDTH:ATTN_WIDTH + 2 * KV_WIDTH],
                 preferred_element_type=F32)
    for hh in range(N_KV_HEADS):
        sl = slice(hh * HEAD_DIM, (hh + 1) * HEAD_DIM)
        vt_ref[0, sl, :] = zv[:, sl].T.astype(BF16)
    u_ref[0] = jnp.dot(h, win_ref[:, ATTN_WIDTH + 2 * KV_WIDTH:], preferred_element_type=F32)


def _proj(x, g_mix, w_in, gq, gk, tables, tm):
    B, L, _ = x.shape
    cos, sup, sdn = tables
    tab_spec = pl.BlockSpec((tm, HEAD_DIM), lambda b, i: (i, 0))
    vec = lambda n: pl.BlockSpec((1, n), lambda b, i: (0, 0))
    return pl.pallas_call(
        _proj_kernel,
        grid=(B, L // tm),
        in_specs=[
            pl.BlockSpec((1, tm, D_MODEL), lambda b, i: (b, i, 0)),
            vec(D_MODEL),
            pl.BlockSpec(w_in.shape, lambda b, i: (0, 0)),
            vec(HEAD_DIM), vec(HEAD_DIM),
            tab_spec, tab_spec, tab_spec,
        ],
        out_specs=[
            pl.BlockSpec((1, ATTN_WIDTH, tm), lambda b, i: (b, 0, i)),
            pl.BlockSpec((1, tm, KV_WIDTH), lambda b, i: (b, i, 0)),
            pl.BlockSpec((1, KV_WIDTH, tm), lambda b, i: (b, 0, i)),
            pl.BlockSpec((1, tm, POOL_WIDTH), lambda b, i: (b, i, 0)),
        ],
        out_shape=[
            jax.ShapeDtypeStruct((B, ATTN_WIDTH, L), BF16),
            jax.ShapeDtypeStruct((B, L, KV_WIDTH), BF16),
            jax.ShapeDtypeStruct((B, KV_WIDTH, L), BF16),
            jax.ShapeDtypeStruct((B, L, POOL_WIDTH), F32),
        ],
        compiler_params=_params("parallel", "parallel"),
        name="proj",
    )(x, g_mix, w_in, gq, gk, cos, sup, sdn)


def _attn_kernel(qt_ref, k_ref, vt_ref, o_ref, acc_ref, s_ref, *, tq, tk, unroll):
    L = k_ref.shape[1]
    n = Q_PER_KV * tq
    nk = L // tk
    nq = L // tq
    assert nk % unroll == 0 and unroll % 2 == 0
    ones_rows = (lax.broadcasted_iota(jnp.int32, (SUBLANES_BF16, tk), 0) == 0).astype(BF16)

    def qcat_at(qi):
        c0 = pl.multiple_of(qi * tq, tq)
        return jnp.concatenate(
            [qt_ref[0, g * HEAD_DIM:(g + 1) * HEAD_DIM, pl.ds(c0, tq)] for g in range(Q_PER_KV)],
            axis=1)

    def scores(qc, j):
        start = pl.multiple_of(j * tk, tk)
        return jnp.dot(k_ref[0, pl.ds(start, tk), :], qc,
                       preferred_element_type=F32)

    def step(j, slot, m, next_q, next_j):
        s_ref[1 - slot] = scores(next_q, next_j)
        st = s_ref[slot]
        m_new = jnp.maximum(m, jnp.max(st, axis=0, keepdims=True))
        alpha = jnp.exp2(m - m_new)
        pt = jnp.exp2(st - m_new).astype(BF16)
        start = pl.multiple_of(j * tk, tk)
        vc = jnp.concatenate([vt_ref[0, :, pl.ds(start, tk)], ones_rows], axis=0)
        acc_ref[...] = alpha * acc_ref[...] + jnp.dot(vc, pt, preferred_element_type=F32)
        return m_new

    s_ref[0] = scores(qcat_at(0), 0)

    def tile(qi, carry):
        qc = qcat_at(qi)
        acc_ref[...] = jnp.zeros_like(acc_ref)

        def group(jj, m):
            for si in range(unroll):
                j = jj * unroll + si
                m = step(j, si % 2, m, qc, j + 1)
            return m

        m = lax.fori_loop(0, nk // unroll - 1, group, jnp.full((1, n), -jnp.inf, F32))
        for si in range(unroll - 1):
            j = nk - unroll + si
            m = step(j, si % 2, m, qc, j + 1)
        step(nk - 1, 1, m, qcat_at(jnp.minimum(qi + 1, nq - 1)), 0)
        out_t = acc_ref[0:HEAD_DIM, :] / acc_ref[HEAD_DIM:HEAD_DIM + 1, :]
        r0 = pl.multiple_of(qi * tq, tq)
        for g in range(Q_PER_KV):
            o_ref[0, pl.ds(r0, tq), g * HEAD_DIM:(g + 1) * HEAD_DIM] = (
                out_t[:, g * tq:(g + 1) * tq].T.astype(BF16))
        return carry

    lax.fori_loop(0, nq, tile, 0)


def _attention(qt, k, vt, tq, tk, unroll):
    B, _, L = qt.shape
    gw = Q_PER_KV * HEAD_DIM
    return pl.pallas_call(
        functools.partial(_attn_kernel, tq=tq, tk=tk, unroll=unroll),
        grid=(B, N_KV_HEADS),
        in_specs=[
            pl.BlockSpec((1, gw, L), lambda b, kh: (b, kh, 0)),
            pl.BlockSpec((1, L, HEAD_DIM), lambda b, kh: (b, 0, kh)),
            pl.BlockSpec((1, HEAD_DIM, L), lambda b, kh: (b, kh, 0)),
        ],
        out_specs=pl.BlockSpec((1, L, gw), lambda b, kh: (b, 0, kh)),
        out_shape=jax.ShapeDtypeStruct((B, L, ATTN_WIDTH), BF16),
        scratch_shapes=[
            pltpu.VMEM((HEAD_DIM + SUBLANES_BF16, Q_PER_KV * tq), F32),
            pltpu.VMEM((2, tk, Q_PER_KV * tq), F32),
        ],
        compiler_params=_params("parallel", "parallel"),
        name="attention",
    )(qt, k, vt)


def _mix_kernel(a_ref, u_ref, uprev_ref, unext_ref, x_ref, wpool_ref, pscale_ref, wout_ref,
                gffn_ref, x1_ref, h2_ref, uext_ref, cat_ref, *, seq_len):
    i = pl.program_id(1)
    last = pl.num_programs(1) - 1
    tm = u_ref.shape[1]
    hal = POOL_HALO
    uext_ref[0:hal, :] = jnp.where(i > 0, uprev_ref[0], 0.0)
    uext_ref[hal:hal + tm, :] = u_ref[0]
    uext_ref[hal + tm:, :] = jnp.where(i < last, unext_ref[0], 0.0)

    t = i * tm + lax.broadcasted_iota(jnp.int32, (tm, 1), 0)
    cat_ref[:, 0:ATTN_WIDTH] = a_ref[0]
    for g, w in enumerate(POOL_WINDOWS):
        c0 = g * POOL_GROUP_WIDTH
        cols = slice(c0, c0 + POOL_GROUP_WIDTH)
        tot = uext_ref[hal - w // 2:hal - w // 2 + tm, cols]
        for d in range(-w // 2 + 1, w // 2):
            tot = tot + uext_ref[hal + d:hal + d + tm, cols]
        cnt = (jnp.minimum(t + w // 2, seq_len) - jnp.maximum(t - w // 2, 0)).astype(F32)
        dm = tot / cnt - uext_ref[hal:hal + tm, cols]
        y = jnp.dot(dm.astype(BF16), wpool_ref[g], preferred_element_type=F32)
        cat_ref[:, ATTN_WIDTH + c0:ATTN_WIDTH + c0 + POOL_GROUP_WIDTH] = (
            y * pscale_ref[:, cols]).astype(BF16)

    x1 = x_ref[0] + jnp.dot(cat_ref[...], wout_ref[...], preferred_element_type=F32)
    x1_ref[0] = x1
    inv = lax.rsqrt(jnp.mean(x1 * x1, axis=-1, keepdims=True) + EPS)
    h2_ref[0] = (x1 * inv * gffn_ref[...]).astype(BF16)


def _mix(a, u, x, w_pool, pool_scale, w_out, g_ffn, tm):
    B, L, _ = x.shape
    nb = tm // POOL_HALO
    nblk = L // POOL_HALO
    return pl.pallas_call(
        functools.partial(_mix_kernel, seq_len=L),
        grid=(B, L // tm),
        in_specs=[
            pl.BlockSpec((1, tm, ATTN_WIDTH), lambda b, i: (b, i, 0)),
            pl.BlockSpec((1, tm, POOL_WIDTH), lambda b, i: (b, i, 0)),
            pl.BlockSpec((1, POOL_HALO, POOL_WIDTH),
                         lambda b, i: (b, jnp.maximum(i * nb - 1, 0), 0)),
            pl.BlockSpec((1, POOL_HALO, POOL_WIDTH),
                         lambda b, i: (b, jnp.minimum((i + 1) * nb, nblk - 1), 0)),
            pl.BlockSpec((1, tm, D_MODEL), lambda b, i: (b, i, 0)),
            pl.BlockSpec(w_pool.shape, lambda b, i: (0, 0, 0)),
            pl.BlockSpec((1, POOL_WIDTH), lambda b, i: (0, 0)),
            pl.BlockSpec(w_out.shape, lambda b, i: (0, 0)),
            pl.BlockSpec((1, D_MODEL), lambda b, i: (0, 0)),
        ],
        out_specs=[
            pl.BlockSpec((1, tm, D_MODEL), lambda b, i: (b, i, 0)),
            pl.BlockSpec((1, tm, D_MODEL), lambda b, i: (b, i, 0)),
        ],
        out_shape=[
            jax.ShapeDtypeStruct((B, L, D_MODEL), F32),
            jax.ShapeDtypeStruct((B, L, D_MODEL), BF16),
        ],
        scratch_shapes=[
            pltpu.VMEM((tm + 2 * POOL_HALO, POOL_WIDTH), F32),
            pltpu.VMEM((tm, MIX_WIDTH), BF16),
        ],
        compiler_params=_params("parallel", "parallel"),
        name="mix_out",
    )(a, u, u, u, x, w_pool, pool_scale, w_out, g_ffn)


def _ffn_kernel(h_ref, hprev_ref, hnext_ref, x1_hbm, wg_ref, wv_ref, wc_ref, bc_ref, wd_ref,
                o_ref, hext_ref, act_ref, x1_sem):
    b = pl.program_id(0)
    i = pl.program_id(1)
    j = pl.program_id(2)
    last_i = pl.num_programs(1) - 1
    nf = pl.num_programs(2) - 1
    tm = h_ref.shape[1]
    hal = CONV_HALO

    def x1_copy():
        row0 = pl.multiple_of(i * tm, tm)
        return pltpu.make_async_copy(x1_hbm.at[b, pl.ds(row0, tm), :], o_ref.at[0], x1_sem)

    def up():
        ge = jnp.dot(hext_ref[...], wg_ref[...], preferred_element_type=F32)
        val = jnp.dot(h_ref[0], wv_ref[...], preferred_element_type=F32)
        n = ge.shape[0]
        prev = pltpu.roll(ge, 1, 0)[hal:hal + tm]
        nxt = pltpu.roll(ge, n - 1, 0)[hal:hal + tm]
        gate = (prev * wc_ref[0:1, :] + ge[hal:hal + tm] * wc_ref[1:2, :]
                + nxt * wc_ref[2:3, :] + bc_ref[...])
        act_ref[...] = (gate * (1.0 / (1.0 + jnp.exp(-gate))) * val).astype(BF16)

    def down():
        o_ref[0] += jnp.dot(act_ref[...], wd_ref[...], preferred_element_type=F32)

    @pl.when(j == 0)
    def _():
        hext_ref[0:hal, :] = jnp.where(i > 0, hprev_ref[0], jnp.zeros_like(hprev_ref[0]))
        hext_ref[hal:hal + tm, :] = h_ref[0]
        hext_ref[hal + tm:, :] = jnp.where(i < last_i, hnext_ref[0], jnp.zeros_like(hnext_ref[0]))
        x1_copy().start()
        up()

    @pl.when(j == 1)
    def _():
        x1_copy().wait()

    @pl.when((j > 0) & (j < nf))
    def _():
        down()
        up()

    @pl.when(j == nf)
    def _():
        down()


def _ffn(h2, x1, w_up, w_conv, b_conv, w_down, tm, tf):
    B, L, _ = x1.shape
    nb = tm // CONV_HALO
    nblk = L // CONV_HALO
    nf = D_FF // tf
    up_j = lambda j: jnp.minimum(j, nf - 1)
    down_j = lambda j: jnp.maximum(j - 1, 0)
    return pl.pallas_call(
        _ffn_kernel,
        grid=(B, L // tm, nf + 1),
        in_specs=[
            pl.BlockSpec((1, tm, D_MODEL), lambda b, i, j: (b, i, 0)),
            pl.BlockSpec((1, CONV_HALO, D_MODEL),
                         lambda b, i, j: (b, jnp.maximum(i * nb - 1, 0), 0)),
            pl.BlockSpec((1, CONV_HALO, D_MODEL),
                         lambda b, i, j: (b, jnp.minimum((i + 1) * nb, nblk - 1), 0)),
            pl.BlockSpec(memory_space=pl.ANY),
            pl.BlockSpec((D_MODEL, tf), lambda b, i, j: (0, up_j(j))),
            pl.BlockSpec((D_MODEL, tf), lambda b, i, j: (0, up_j(j) + nf)),
            pl.BlockSpec((3, tf), lambda b, i, j: (0, up_j(j))),
            pl.BlockSpec((1, tf), lambda b, i, j: (0, up_j(j))),
            pl.BlockSpec((tf, D_MODEL), lambda b, i, j: (down_j(j), 0)),
        ],
        out_specs=pl.BlockSpec((1, tm, D_MODEL), lambda b, i, j: (b, i, 0)),
        out_shape=jax.ShapeDtypeStruct((B, L, D_MODEL), F32),
        scratch_shapes=[
            pltpu.VMEM((tm + 2 * CONV_HALO, D_MODEL), BF16),
            pltpu.VMEM((tm, tf), BF16),
            pltpu.SemaphoreType.DMA,
        ],
        compiler_params=_params("arbitrary", "arbitrary", "arbitrary"),
        name="ffn",
    )(h2, h2, h2, x1, w_up, w_up, w_conv, b_conv, w_down)


def _tiles(L):
    tm = min(512, L)
    tq = min(256, L)
    tk = min(512, L)
    tf = 512
    attn_unroll = 4
    tm_ffn = min(1024, L)
    return tm, tq, tk, tf, attn_unroll, tm_ffn


def _layer(x, p):
    B, L, _ = x.shape
    tm, tq, tk, tf, attn_unroll, tm_ffn = _tiles(L)
    tables = _rope_tables(L)
    qt, k, vt, u = _proj(x, p["g_mix"], p["w_in"], p["gq"], p["gk"], tables, tm)
    a = _attention(qt, k, vt, tq, tk, attn_unroll)
    x1, h2 = _mix(a, u, x, p["w_pool"], p["pool_scale"], p["w_out"], p["g_ffn"], tm)
    return _ffn(h2, x1, p["w_up"], p["w_conv"], p["b_conv"], p["w_down"], tm_ffn, tf)


def _prep(i, g_norm_mix, w_in, g_q, g_k, w_pool, pool_scale, w_out, g_norm_ffn, w_up, w_conv,
          b_conv, w_down):
    q_scale = HEAD_DIM ** -0.5 * math.log2(math.e)
    return dict(
        g_mix=g_norm_mix[i][None, :],
        w_in=w_in[i].astype(BF16),
        gq=(g_q[i] * q_scale)[None, :],
        gk=g_k[i][None, :],
        w_pool=w_pool[i].astype(BF16),
        pool_scale=pool_scale[i][None, :],
        w_out=w_out[i].astype(BF16),
        g_ffn=g_norm_ffn[i][None, :],
        w_up=w_up[i].astype(BF16),
        w_conv=w_conv[i],
        b_conv=b_conv[i][None, :],
        w_down=w_down[i].astype(BF16),
    )


def kernel(x_prompt, x_sample, g_norm_mix, w_in, g_q, g_k, w_pool, pool_scale, w_out,
           g_norm_ffn, w_up, w_conv, b_conv, w_down):
    depth = w_in.shape[0]
    layers = [_prep(i, g_norm_mix, w_in, g_q, g_k, w_pool, pool_scale, w_out, g_norm_ffn,
                    w_up, w_conv, b_conv, w_down) for i in range(depth)]
    outs = []
    for x in (x_prompt, x_sample):
        for p in layers:
            x = _layer(x, p)
        outs.append(x)
    return tuple(outs)
```

```python
import functools
import math

import jax
import jax.numpy as jnp
from jax import lax
from jax.experimental import pallas as pl
from jax.experimental.pallas import tpu as pltpu

D_MODEL = 2048
GRID_W = 64
HEAD_DIM = 128
N_Q_HEADS = 8
N_KV_HEADS = 2
Q_PER_KV = N_Q_HEADS // N_KV_HEADS
ATTN_WIDTH = N_Q_HEADS * HEAD_DIM
KV_WIDTH = N_KV_HEADS * HEAD_DIM
POOL_WINDOWS = (2, 4, 8, 16)
POOL_GROUP_WIDTH = 256
POOL_WIDTH = len(POOL_WINDOWS) * POOL_GROUP_WIDTH
MIX_WIDTH = ATTN_WIDTH + POOL_WIDTH
D_FF = 5632
ROPE_THETA = 10000.0
ROPE_AXIS_DIM = HEAD_DIM // 2
EPS = 1e-6

F32 = jnp.float32
BF16 = jnp.bfloat16

V7X_VMEM_BYTES = 64 * 1024 * 1024
SUBLANES_F32 = 8
SUBLANES_BF16 = 16
POOL_HALO = max(POOL_WINDOWS) // 2
assert POOL_HALO == SUBLANES_F32
CONV_HALO = SUBLANES_BF16

VMEM_LIMIT = 56 * 1024 * 1024


def _params(*sem):
    return pltpu.CompilerParams(dimension_semantics=sem, vmem_limit_bytes=VMEM_LIMIT)


def _rope_tables(L):
    t = jnp.arange(L)
    row = (t // GRID_W).astype(F32)
    col = (t % GRID_W).astype(F32)
    inv_freq = ROPE_THETA ** (-jnp.arange(0, ROPE_AXIS_DIM, 2, dtype=F32) / ROPE_AXIS_DIM)
    ang_r = row[:, None] * inv_freq
    ang_c = col[:, None] * inv_freq
    ang = jnp.concatenate([ang_r, ang_r, ang_c, ang_c], axis=1)
    cos, sin = jnp.cos(ang), jnp.sin(ang)
    first = (jnp.arange(HEAD_DIM) % ROPE_AXIS_DIM) < (ROPE_AXIS_DIM // 2)
    sin_up = jnp.where(first, -sin, 0.0)
    sin_dn = jnp.where(first, 0.0, sin)
    return cos, sin_up, sin_dn


def _proj_kernel(x_ref, gmix_ref, win_ref, gq_ref, gk_ref, cos_ref, sup_ref, sdn_ref,
                 qt_ref, k_ref, vt_ref, u_ref, *, ts):
    tm = x_ref.shape[1]
    for r0 in range(0, tm, ts):
        rows = slice(r0, r0 + ts)
        x = x_ref[0, rows, :]
        inv = lax.rsqrt(jnp.mean(x * x, axis=-1, keepdims=True) + EPS)
        h = (x * inv * gmix_ref[...]).astype(BF16)
        cos, sup, sdn = cos_ref[rows, :], sup_ref[rows, :], sdn_ref[rows, :]

        def norm_rope(zh, g):
            r = lax.rsqrt(jnp.mean(zh * zh, axis=-1, keepdims=True) + EPS)
            n = zh * r * g
            up = pltpu.roll(n, HEAD_DIM - ROPE_AXIS_DIM // 2, 1)
            dn = pltpu.roll(n, ROPE_AXIS_DIM // 2, 1)
            return n * cos + up * sup + dn * sdn

        zq = jnp.dot(h, win_ref[:, 0:ATTN_WIDTH], preferred_element_type=F32)
        for hh in range(N_Q_HEADS):
            sl = slice(hh * HEAD_DIM, (hh + 1) * HEAD_DIM)
            qt_ref[0, sl, rows] = norm_rope(zq[:, sl], gq_ref[...]).T.astype(BF16)
        zk = jnp.dot(h, win_ref[:, ATTN_WIDTH:ATTN_WIDTH + KV_WIDTH], preferred_element_type=F32)
        for hh in range(N_KV_HEADS):
            sl = slice(hh * HEAD_DIM, (hh + 1) * HEAD_DIM)
            k_ref[0, rows, sl] = norm_rope(zk[:, sl], gk_ref[...]).astype(BF16)
        zv = jnp.dot(h, win_ref[:, ATTN_WIDTH + KV_WIDTH:ATTN_WIDTH + 2 * KV_WIDTH],
                     preferred_element_type=F32)
        for hh in range(N_KV_HEADS):
            sl = slice(hh * HEAD_DIM, (hh + 1) * HEAD_DIM)
            vt_ref[0, sl, rows] = zv[:, sl].T.astype(BF16)
        u_ref[0, rows, :] = jnp.dot(h, win_ref[:, ATTN_WIDTH + 2 * KV_WIDTH:],
                                    preferred_element_type=F32)


def _proj(x, g_mix, w_in, gq, gk, tables, tm, ts):
    B, L, _ = x.shape
    cos, sup, sdn = tables
    tab_spec = pl.BlockSpec((tm, HEAD_DIM), lambda b, i: (i, 0))
    vec = lambda n: pl.BlockSpec((1, n), lambda b, i: (0, 0))
    return pl.pallas_call(
        functools.partial(_proj_kernel, ts=ts),
        grid=(B, L // tm),
        in_specs=[
            pl.BlockSpec((1, tm, D_MODEL), lambda b, i: (b, i, 0)),
            vec(D_MODEL),
            pl.BlockSpec(w_in.shape, lambda b, i: (0, 0)),
            vec(HEAD_DIM), vec(HEAD_DIM),
            tab_spec, tab_spec, tab_spec,
        ],
        out_specs=[
            pl.BlockSpec((1, ATTN_WIDTH, tm), lambda b, i: (b, 0, i)),
            pl.BlockSpec((1, tm, KV_WIDTH), lambda b, i: (b, i, 0)),
            pl.BlockSpec((1, KV_WIDTH, tm), lambda b, i: (b, 0, i)),
            pl.BlockSpec((1, tm, POOL_WIDTH), lambda b, i: (b, i, 0)),
        ],
        out_shape=[
            jax.ShapeDtypeStruct((B, ATTN_WIDTH, L), BF16),
            jax.ShapeDtypeStruct((B, L, KV_WIDTH), BF16),
            jax.ShapeDtypeStruct((B, KV_WIDTH, L), BF16),
            jax.ShapeDtypeStruct((B, L, POOL_WIDTH), F32),
        ],
        compiler_params=_params("parallel", "parallel"),
        name="proj",
    )(x, g_mix, w_in, gq, gk, cos, sup, sdn)


def _attn_kernel(qt_ref, k_ref, vt_ref, o_ref, acc_ref, s_ref, *, tq, tk, unroll):
    L = k_ref.shape[1]
    n = Q_PER_KV * tq
    nk = L // tk
    nq = L // tq
    assert nk % unroll == 0 and unroll % 2 == 0
    ones_rows = (lax.broadcasted_iota(jnp.int32, (SUBLANES_BF16, tk), 0) == 0).astype(BF16)

    def qcat_at(qi):
        c0 = pl.multiple_of(qi * tq, tq)
        return jnp.concatenate(
            [qt_ref[0, g * HEAD_DIM:(g + 1) * HEAD_DIM, pl.ds(c0, tq)] for g in range(Q_PER_KV)],
            axis=1)

    def scores(qc, j):
        start = pl.multiple_of(j * tk, tk)
        return jnp.dot(k_ref[0, pl.ds(start, tk), :], qc,
                       preferred_element_type=F32)

    def issue_scores(slot, qc, j):
        s = scores(qc, j)
        s_ref[slot] = s
        return jnp.max(s, axis=0, keepdims=True)

    def step(j, slot, m, cmax, next_q, next_j):
        cmax_next = issue_scores(1 - slot, next_q, next_j)
        m_new = jnp.maximum(m, cmax)
        alpha = jnp.exp2(m - m_new)
        pt = jnp.exp2(s_ref[slot] - m_new).astype(BF16)
        start = pl.multiple_of(j * tk, tk)
        vc = jnp.concatenate([vt_ref[0, :, pl.ds(start, tk)], ones_rows], axis=0)
        acc_ref[...] = alpha * acc_ref[...] + jnp.dot(vc, pt, preferred_element_type=F32)
        return m_new, cmax_next

    def tile(qi, cmax):
        qc = qcat_at(qi)
        acc_ref[...] = jnp.zeros_like(acc_ref)

        def group(jj, carry):
            for si in range(unroll):
                j = jj * unroll + si
                carry = step(j, si % 2, *carry, qc, j + 1)
            return carry

        carry = lax.fori_loop(0, nk // unroll - 1, group,
                              (jnp.full((1, n), -jnp.inf, F32), cmax))
        for si in range(unroll - 1):
            j = nk - unroll + si
            carry = step(j, si % 2, *carry, qc, j + 1)
        _, cmax = step(nk - 1, 1, *carry, qcat_at(jnp.minimum(qi + 1, nq - 1)), 0)
        out_t = acc_ref[0:HEAD_DIM, :] / acc_ref[HEAD_DIM:HEAD_DIM + 1, :]
        r0 = pl.multiple_of(qi * tq, tq)
        for g in range(Q_PER_KV):
            o_ref[0, pl.ds(r0, tq), g * HEAD_DIM:(g + 1) * HEAD_DIM] = (
                out_t[:, g * tq:(g + 1) * tq].T.astype(BF16))
        return cmax

    lax.fori_loop(0, nq, tile, issue_scores(0, qcat_at(0), 0))


def _attention(qt, k, vt, tq, tk, unroll):
    B, _, L = qt.shape
    gw = Q_PER_KV * HEAD_DIM
    return pl.pallas_call(
        functools.partial(_attn_kernel, tq=tq, tk=tk, unroll=unroll),
        grid=(B, N_KV_HEADS),
        in_specs=[
            pl.BlockSpec((1, gw, L), lambda b, kh: (b, kh, 0)),
            pl.BlockSpec((1, L, HEAD_DIM), lambda b, kh: (b, 0, kh)),
            pl.BlockSpec((1, HEAD_DIM, L), lambda b, kh: (b, kh, 0)),
        ],
        out_specs=pl.BlockSpec((1, L, gw), lambda b, kh: (b, 0, kh)),
        out_shape=jax.ShapeDtypeStruct((B, L, ATTN_WIDTH), BF16),
        scratch_shapes=[
            pltpu.VMEM((HEAD_DIM + SUBLANES_BF16, Q_PER_KV * tq), F32),
            pltpu.VMEM((2, tk, Q_PER_KV * tq), F32),
        ],
        compiler_params=_params("parallel", "parallel"),
        name="attention",
    )(qt, k, vt)


def _mix_kernel(a_ref, u_ref, uprev_ref, unext_ref, x_ref, wpool_ref, pscale_ref, wout_ref,
                gffn_ref, x1_ref, h2_ref, *, seq_len, ts):
    i = pl.program_id(1)
    last = pl.num_programs(1) - 1
    tm = u_ref.shape[1]
    hal = POOL_HALO

    def pool(r0):
        rows = slice(r0, r0 + ts)
        t = i * tm + r0 + lax.broadcasted_iota(jnp.int32, (ts, 1), 0)
        pooled = []
        for g, w in enumerate(POOL_WINDOWS):
            c0 = g * POOL_GROUP_WIDTH
            cols = slice(c0, c0 + POOL_GROUP_WIDTH)
            before = (u_ref[0, r0 - hal:r0, cols] if r0 > 0
                      else jnp.where(i > 0, uprev_ref[0, :, cols], 0.0))
            after = (u_ref[0, r0 + ts:r0 + ts + hal, cols] if r0 + ts < tm
                     else jnp.where(i < last, unext_ref[0, :, cols], 0.0))
            ue = jnp.concatenate([before, u_ref[0, rows, cols], after], axis=0)
            n = ue.shape[0]
            tot, span = ue, 1
            while span < w:
                tot = tot + pltpu.roll(tot, span, 0)
                span *= 2
            if w // 2 - 1:
                tot = pltpu.roll(tot, n - (w // 2 - 1), 0)
            cnt = (jnp.minimum(t + w // 2, seq_len) - jnp.maximum(t - w // 2, 0)).astype(F32)
            dm = tot[hal:hal + ts] / cnt - ue[hal:hal + ts]
            y = jnp.dot(dm.astype(BF16), wpool_ref[g], preferred_element_type=F32)
            pooled.append((y * pscale_ref[:, cols]).astype(BF16))
        return jnp.concatenate(pooled, axis=1)

    starts = list(range(0, tm, ts))
    pooled = pool(starts[0])
    for s, r0 in enumerate(starts):
        rows = slice(r0, r0 + ts)
        acc = x_ref[0, rows, :] + jnp.dot(a_ref[0, rows, :], wout_ref[0:ATTN_WIDTH, :],
                                          preferred_element_type=F32)
        pooled_next = pool(starts[s + 1]) if s + 1 < len(starts) else None
        x1 = acc + jnp.dot(pooled, wout_ref[ATTN_WIDTH:, :], preferred_element_type=F32)
        x1_ref[0, rows, :] = x1
        inv = lax.rsqrt(jnp.mean(x1 * x1, axis=-1, keepdims=True) + EPS)
        h2_ref[0, rows, :] = (x1 * inv * gffn_ref[...]).astype(BF16)
        pooled = pooled_next


def _mix(a, u, x, w_pool, pool_scale, w_out, g_ffn, tm, ts):
    B, L, _ = x.shape
    nb = tm // POOL_HALO
    nblk = L // POOL_HALO
    return pl.pallas_call(
        functools.partial(_mix_kernel, seq_len=L, ts=ts),
        grid=(B, L // tm),
        in_specs=[
            pl.BlockSpec((1, tm, ATTN_WIDTH), lambda b, i: (b, i, 0)),
            pl.BlockSpec((1, tm, POOL_WIDTH), lambda b, i: (b, i, 0)),
            pl.BlockSpec((1, POOL_HALO, POOL_WIDTH),
                         lambda b, i: (b, jnp.maximum(i * nb - 1, 0), 0)),
            pl.BlockSpec((1, POOL_HALO, POOL_WIDTH),
                         lambda b, i: (b, jnp.minimum((i + 1) * nb, nblk - 1), 0)),
            pl.BlockSpec((1, tm, D_MODEL), lambda b, i: (b, i, 0)),
            pl.BlockSpec(w_pool.shape, lambda b, i: (0, 0, 0)),
            pl.BlockSpec((1, POOL_WIDTH), lambda b, i: (0, 0)),
            pl.BlockSpec(w_out.shape, lambda b, i: (0, 0)),
            pl.BlockSpec((1, D_MODEL), lambda b, i: (0, 0)),
        ],
        out_specs=[
            pl.BlockSpec((1, tm, D_MODEL), lambda b, i: (b, i, 0)),
            pl.BlockSpec((1, tm, D_MODEL), lambda b, i: (b, i, 0)),
        ],
        out_shape=[
            jax.ShapeDtypeStruct((B, L, D_MODEL), F32),
            jax.ShapeDtypeStruct((B, L, D_MODEL), BF16),
        ],
        compiler_params=_params("parallel", "parallel"),
        name="mix_out",
    )(a, u, u, u, x, w_pool, pool_scale, w_out, g_ffn)


def _ffn_kernel(h_ref, hprev_ref, hnext_ref, x1_hbm, wg_ref, wv_ref, wc_ref, bc_ref, wd_ref,
                o_ref, hext_ref, act_ref, x1_sem):
    b = pl.program_id(0)
    i = pl.program_id(1)
    j = pl.program_id(2)
    last_i = pl.num_programs(1) - 1
    nf = pl.num_programs(2) - 1
    tm = h_ref.shape[1]
    hal = CONV_HALO

    def x1_copy():
        row0 = pl.multiple_of(i * tm, tm)
        return pltpu.make_async_copy(x1_hbm.at[b, pl.ds(row0, tm), :], o_ref.at[0], x1_sem)

    def up():
        ge = jnp.dot(hext_ref[...], wg_ref[...], preferred_element_type=F32)
        val = jnp.dot(h_ref[0], wv_ref[...], preferred_element_type=F32)
        n = ge.shape[0]
        prev = pltpu.roll(ge, 1, 0)[hal:hal + tm]
        nxt = pltpu.roll(ge, n - 1, 0)[hal:hal + tm]
        gate = (prev * wc_ref[0:1, :] + ge[hal:hal + tm] * wc_ref[1:2, :]
                + nxt * wc_ref[2:3, :] + bc_ref[...])
        act_ref[...] = (gate * (1.0 / (1.0 + jnp.exp(-gate))) * val).astype(BF16)

    def down():
        o_ref[0] += jnp.dot(act_ref[...], wd_ref[...], preferred_element_type=F32)

    @pl.when(j == 0)
    def _():
        hext_ref[0:hal, :] = jnp.where(i > 0, hprev_ref[0], jnp.zeros_like(hprev_ref[0]))
        hext_ref[hal:hal + tm, :] = h_ref[0]
        hext_ref[hal + tm:, :] = jnp.where(i < last_i, hnext_ref[0], jnp.zeros_like(hnext_ref[0]))
        x1_copy().start()
        up()

    @pl.when(j == 1)
    def _():
        x1_copy().wait()

    @pl.when((j > 0) & (j < nf))
    def _():
        down()
        up()

    @pl.when(j == nf)
    def _():
        down()


def _ffn(h2, x1, w_up, w_conv, b_conv, w_down, tm, tf):
    B, L, _ = x1.shape
    nb = tm // CONV_HALO
    nblk = L // CONV_HALO
    nf = D_FF // tf
    up_j = lambda j: jnp.minimum(j, nf - 1)
    down_j = lambda j: jnp.maximum(j - 1, 0)
    return pl.pallas_call(
        _ffn_kernel,
        grid=(B, L // tm, nf + 1),
        in_specs=[
            pl.BlockSpec((1, tm, D_MODEL), lambda b, i, j: (b, i, 0)),
            pl.BlockSpec((1, CONV_HALO, D_MODEL),
                         lambda b, i, j: (b, jnp.maximum(i * nb - 1, 0), 0)),
            pl.BlockSpec((1, CONV_HALO, D_MODEL),
                         lambda b, i, j: (b, jnp.minimum((i + 1) * nb, nblk - 1), 0)),
            pl.BlockSpec(memory_space=pl.ANY),
            pl.BlockSpec((D_MODEL, tf), lambda b, i, j: (0, up_j(j))),
            pl.BlockSpec((D_MODEL, tf), lambda b, i, j: (0, up_j(j) + nf)),
            pl.BlockSpec((3, tf), lambda b, i, j: (0, up_j(j))),
            pl.BlockSpec((1, tf), lambda b, i, j: (0, up_j(j))),
            pl.BlockSpec((tf, D_MODEL), lambda b, i, j: (down_j(j), 0)),
        ],
        out_specs=pl.BlockSpec((1, tm, D_MODEL), lambda b, i, j: (b, i, 0)),
        out_shape=jax.ShapeDtypeStruct((B, L, D_MODEL), F32),
        scratch_shapes=[
            pltpu.VMEM((tm + 2 * CONV_HALO, D_MODEL), BF16),
            pltpu.VMEM((tm, tf), BF16),
            pltpu.SemaphoreType.DMA,
        ],
        compiler_params=_params("arbitrary", "arbitrary", "arbitrary"),
        name="ffn",
    )(h2, h2, h2, x1, w_up, w_up, w_conv, b_conv, w_down)


def _tiles(L):
    tm = min(512, L)
    tq = min(256, L)
    tk = min(512, L)
    tf = 512
    attn_unroll = 8
    tm_ffn = min(1024, L)
    ts = min(256, tm)
    return tm, tq, tk, tf, attn_unroll, tm_ffn, ts


def _layer(x, p):
    B, L, _ = x.shape
    tm, tq, tk, tf, attn_unroll, tm_ffn, ts = _tiles(L)
    tables = _rope_tables(L)
    qt, k, vt, u = _proj(x, p["g_mix"], p["w_in"], p["gq"], p["gk"], tables, tm, ts)
    a = _attention(qt, k, vt, tq, tk, attn_unroll)
    x1, h2 = _mix(a, u, x, p["w_pool"], p["pool_scale"], p["w_out"], p["g_ffn"], tm, ts)
    return _ffn(h2, x1, p["w_up"], p["w_conv"], p["b_conv"], p["w_down"], tm_ffn, tf)


def _prep(i, g_norm_mix, w_in, g_q, g_k, w_pool, pool_scale, w_out, g_norm_ffn, w_up, w_conv,
          b_conv, w_down):
    q_scale = HEAD_DIM ** -0.5 * math.log2(math.e)
    return dict(
        g_mix=g_norm_mix[i][None, :],
        w_in=w_in[i].astype(BF16),
        gq=(g_q[i] * q_scale)[None, :],
        gk=g_k[i][None, :],
        w_pool=w_pool[i].astype(BF16),
        pool_scale=pool_scale[i][None, :],
        w_out=w_out[i].astype(BF16),
        g_ffn=g_norm_ffn[i][None, :],
        w_up=w_up[i].astype(BF16),
        w_conv=w_conv[i],
        b_conv=b_conv[i][None, :],
        w_down=w_down[i].astype(BF16),
    )


def kernel(x_prompt, x_sample, g_norm_mix, w_in, g_q, g_k, w_pool, pool_scale, w_out,
           g_norm_ffn, w_up, w_conv, b_conv, w_down):
    depth = w_in.shape[0]
    layers = [_prep(i, g_norm_mix, w_in, g_q, g_k, w_pool, pool_scale, w_out, g_norm_ffn,
                    w_up, w_conv, b_conv, w_down) for i in range(depth)]
    outs = []
    for x in (x_prompt, x_sample):
        for p in layers:
            x = _layer(x, p)
        outs.append(x)
    return tuple(outs)
```

```python
import functools
import math

import jax
import jax.numpy as jnp
from jax import lax
from jax.experimental import pallas as pl
from jax.experimental.pallas import tpu as pltpu

D_MODEL = 2048
GRID_W = 64
HEAD_DIM = 128
N_Q_HEADS = 8
N_KV_HEADS = 2
Q_PER_KV = N_Q_HEADS // N_KV_HEADS
ATTN_WIDTH = N_Q_HEADS * HEAD_DIM
KV_WIDTH = N_KV_HEADS * HEAD_DIM
POOL_WINDOWS = (2, 4, 8, 16)
POOL_GROUP_WIDTH = 256
POOL_WIDTH = len(POOL_WINDOWS) * POOL_GROUP_WIDTH
MIX_WIDTH = ATTN_WIDTH + POOL_WIDTH
D_FF = 5632
ROPE_THETA = 10000.0
ROPE_AXIS_DIM = HEAD_DIM // 2
EPS = 1e-6

F32 = jnp.float32
BF16 = jnp.bfloat16

V7X_VMEM_BYTES = 64 * 1024 * 1024
SUBLANES_F32 = 8
SUBLANES_BF16 = 16
POOL_HALO = max(POOL_WINDOWS) // 2
assert POOL_HALO == SUBLANES_F32
CONV_HALO = SUBLANES_BF16

VMEM_LIMIT = 56 * 1024 * 1024


def _params(*sem):
    return pltpu.CompilerParams(dimension_semantics=sem, vmem_limit_bytes=VMEM_LIMIT)


def _rope_tables(L):
    n_rows = L // GRID_W
    half = ROPE_AXIS_DIM // 2
    inv_freq = ROPE_THETA ** (-jnp.arange(0, ROPE_AXIS_DIM, 2, dtype=F32) / ROPE_AXIS_DIM)
    ang_r = jnp.arange(n_rows, dtype=F32)[:, None] * inv_freq
    ang_c = jnp.arange(GRID_W, dtype=F32)[:, None] * inv_freq
    by_row = lambda v: jnp.broadcast_to(v[:, None, :], (n_rows, GRID_W, half))
    by_col = lambda v: jnp.broadcast_to(v[None, :, :], (n_rows, GRID_W, half))
    cos_r, sin_r = by_row(jnp.cos(ang_r)), by_row(jnp.sin(ang_r))
    cos_c, sin_c = by_col(jnp.cos(ang_c)), by_col(jnp.sin(ang_c))
    zero = jnp.zeros((n_rows, GRID_W, half), F32)
    table = lambda parts: jnp.concatenate(parts, axis=-1).reshape(L, HEAD_DIM)
    cos = table([cos_r, cos_r, cos_c, cos_c])
    sin_up = table([-sin_r, zero, -sin_c, zero])
    sin_dn = table([zero, sin_r, zero, sin_c])
    return cos, sin_up, sin_dn


def _proj_kernel(x_ref, gmix_ref, win_ref, gq_ref, gk_ref, cos_ref, sup_ref, sdn_ref,
                 qt_ref, k_ref, vt_ref, u_ref, *, ts):
    tm = x_ref.shape[1]
    for r0 in range(0, tm, ts):
        rows = slice(r0, r0 + ts)
        x = x_ref[0, rows, :]
        inv = lax.rsqrt(jnp.mean(x * x, axis=-1, keepdims=True) + EPS)
        h = (x * inv * gmix_ref[...]).astype(BF16)
        cos, sup, sdn = cos_ref[rows, :], sup_ref[rows, :], sdn_ref[rows, :]

        def norm_rope(zh, g):
            r = lax.rsqrt(jnp.mean(zh * zh, axis=-1, keepdims=True) + EPS)
            n = zh * r * g
            up = pltpu.roll(n, HEAD_DIM - ROPE_AXIS_DIM // 2, 1)
            dn = pltpu.roll(n, ROPE_AXIS_DIM // 2, 1)
            return n * cos + up * sup + dn * sdn

        zq = jnp.dot(h, win_ref[:, 0:ATTN_WIDTH], preferred_element_type=F32)
        for hh in range(N_Q_HEADS):
            sl = slice(hh * HEAD_DIM, (hh + 1) * HEAD_DIM)
            qt_ref[0, sl, rows] = norm_rope(zq[:, sl], gq_ref[...]).T.astype(BF16)
        zk = jnp.dot(h, win_ref[:, ATTN_WIDTH:ATTN_WIDTH + KV_WIDTH], preferred_element_type=F32)
        for hh in range(N_KV_HEADS):
            sl = slice(hh * HEAD_DIM, (hh + 1) * HEAD_DIM)
            k_ref[0, rows, sl] = norm_rope(zk[:, sl], gk_ref[...]).astype(BF16)
        zv = jnp.dot(h, win_ref[:, ATTN_WIDTH + KV_WIDTH:ATTN_WIDTH + 2 * KV_WIDTH],
                     preferred_element_type=F32)
        for hh in range(N_KV_HEADS):
            sl = slice(hh * HEAD_DIM, (hh + 1) * HEAD_DIM)
            vt_ref[0, sl, rows] = zv[:, sl].T.astype(BF16)
        u_ref[0, rows, :] = jnp.dot(h, win_ref[:, ATTN_WIDTH + 2 * KV_WIDTH:],
                                    preferred_element_type=F32)


def _proj(x, g_mix, w_in, gq, gk, tables, tm, ts):
    B, L, _ = x.shape
    cos, sup, sdn = tables
    tab_spec = pl.BlockSpec((tm, HEAD_DIM), lambda b, i: (i, 0))
    vec = lambda n: pl.BlockSpec((1, n), lambda b, i: (0, 0))
    return pl.pallas_call(
        functools.partial(_proj_kernel, ts=ts),
        grid=(B, L // tm),
        in_specs=[
            pl.BlockSpec((1, tm, D_MODEL), lambda b, i: (b, i, 0)),
            vec(D_MODEL),
            pl.BlockSpec(w_in.shape, lambda b, i: (0, 0)),
            vec(HEAD_DIM), vec(HEAD_DIM),
            tab_spec, tab_spec, tab_spec,
        ],
        out_specs=[
            pl.BlockSpec((1, ATTN_WIDTH, tm), lambda b, i: (b, 0, i)),
            pl.BlockSpec((1, tm, KV_WIDTH), lambda b, i: (b, i, 0)),
            pl.BlockSpec((1, KV_WIDTH, tm), lambda b, i: (b, 0, i)),
            pl.BlockSpec((1, tm, POOL_WIDTH), lambda b, i: (b, i, 0)),
        ],
        out_shape=[
            jax.ShapeDtypeStruct((B, ATTN_WIDTH, L), BF16),
            jax.ShapeDtypeStruct((B, L, KV_WIDTH), BF16),
            jax.ShapeDtypeStruct((B, KV_WIDTH, L), BF16),
            jax.ShapeDtypeStruct((B, L, POOL_WIDTH), F32),
        ],
        compiler_params=_params("parallel", "parallel"),
        name="proj",
    )(x, g_mix, w_in, gq, gk, cos, sup, sdn)


def _attn_kernel(qt_ref, k_ref, vt_ref, o_ref, acc_ref, s_ref, *, tq, tk, unroll):
    L = k_ref.shape[1]
    n = Q_PER_KV * tq
    nk = L // tk
    nq = L // tq
    assert nk % unroll == 0 and unroll % 2 == 0
    ones_rows = (lax.broadcasted_iota(jnp.int32, (SUBLANES_BF16, tk), 0) == 0).astype(BF16)

    def qcat_at(qi):
        c0 = pl.multiple_of(qi * tq, tq)
        return jnp.concatenate(
            [qt_ref[0, g * HEAD_DIM:(g + 1) * HEAD_DIM, pl.ds(c0, tq)] for g in range(Q_PER_KV)],
            axis=1)

    def scores(qc, j):
        start = pl.multiple_of(j * tk, tk)
        return jnp.dot(k_ref[0, pl.ds(start, tk), :], qc,
                       preferred_element_type=F32)

    def issue_scores(slot, qc, j):
        s = scores(qc, j)
        s_ref[slot] = s
        return jnp.max(s, axis=0, keepdims=True)

    def step(j, slot, aslot, m, cmax, next_q, next_j):
        cmax_next = issue_scores(1 - slot, next_q, next_j)
        m_new = jnp.maximum(m, cmax)
        alpha = jnp.exp2(m - m_new)
        pt = jnp.exp2(s_ref[slot] - m_new).astype(BF16)
        start = pl.multiple_of(j * tk, tk)
        vc = jnp.concatenate([vt_ref[0, :, pl.ds(start, tk)], ones_rows], axis=0)
        acc_ref[aslot] = alpha * acc_ref[aslot] + jnp.dot(vc, pt, preferred_element_type=F32)
        return m_new, cmax_next

    def finish(qi, aslot):
        out_t = acc_ref[aslot, 0:HEAD_DIM, :] / acc_ref[aslot, HEAD_DIM:HEAD_DIM + 1, :]
        r0 = pl.multiple_of(qi * tq, tq)
        for g in range(Q_PER_KV):
            o_ref[0, pl.ds(r0, tq), g * HEAD_DIM:(g + 1) * HEAD_DIM] = (
                out_t[:, g * tq:(g + 1) * tq].T.astype(BF16))

    def tile(qi, cmax):
        qc = qcat_at(qi)
        aslot = lax.rem(qi, 2)
        acc_ref[aslot] = jnp.zeros(acc_ref.shape[1:], F32)

        def group(jj, carry):
            for si in range(unroll):
                j = jj * unroll + si
                carry = step(j, si % 2, aslot, *carry, qc, j + 1)
            return carry

        carry = lax.fori_loop(0, nk // unroll - 1, group,
                              (jnp.full((1, n), -jnp.inf, F32), cmax))
        for si in range(unroll - 1):
            j = nk - unroll + si
            carry = step(j, si % 2, aslot, *carry, qc, j + 1)
            if si == 0:
                finish(jnp.maximum(qi - 1, 0), 1 - aslot)
        _, cmax = step(nk - 1, 1, aslot, *carry, qcat_at(jnp.minimum(qi + 1, nq - 1)), 0)
        return cmax

    acc_ref[1] = jnp.ones(acc_ref.shape[1:], F32)
    lax.fori_loop(0, nq, tile, issue_scores(0, qcat_at(0), 0))
    finish(nq - 1, (nq - 1) % 2)


def _attention(qt, k, vt, tq, tk, unroll):
    B, _, L = qt.shape
    gw = Q_PER_KV * HEAD_DIM
    return pl.pallas_call(
        functools.partial(_attn_kernel, tq=tq, tk=tk, unroll=unroll),
        grid=(B, N_KV_HEADS),
        in_specs=[
            pl.BlockSpec((1, gw, L), lambda b, kh: (b, kh, 0)),
            pl.BlockSpec((1, L, HEAD_DIM), lambda b, kh: (b, 0, kh)),
            pl.BlockSpec((1, HEAD_DIM, L), lambda b, kh: (b, kh, 0)),
        ],
        out_specs=pl.BlockSpec((1, L, gw), lambda b, kh: (b, 0, kh)),
        out_shape=jax.ShapeDtypeStruct((B, L, ATTN_WIDTH), BF16),
        scratch_shapes=[
            pltpu.VMEM((2, HEAD_DIM + SUBLANES_BF16, Q_PER_KV * tq), F32),
            pltpu.VMEM((2, tk, Q_PER_KV * tq), F32),
        ],
        compiler_params=_params("parallel", "parallel"),
        name="attention",
    )(qt, k, vt)


def _mix_kernel(a_ref, u_ref, uprev_ref, unext_ref, x_ref, wpool_ref, pscale_ref, wout_ref,
                gffn_ref, x1_ref, h2_ref, *, seq_len, ts):
    i = pl.program_id(1)
    last = pl.num_programs(1) - 1
    tm = u_ref.shape[1]
    hal = POOL_HALO

    def pool(r0):
        rows = slice(r0, r0 + ts)
        t = i * tm + r0 + lax.broadcasted_iota(jnp.int32, (ts, 1), 0)
        pooled = []
        for g, w in enumerate(POOL_WINDOWS):
            c0 = g * POOL_GROUP_WIDTH
            cols = slice(c0, c0 + POOL_GROUP_WIDTH)
            before = (u_ref[0, r0 - hal:r0, cols] if r0 > 0
                      else jnp.where(i > 0, uprev_ref[0, :, cols], 0.0))
            after = (u_ref[0, r0 + ts:r0 + ts + hal, cols] if r0 + ts < tm
                     else jnp.where(i < last, unext_ref[0, :, cols], 0.0))
            ue = jnp.concatenate([before, u_ref[0, rows, cols], after], axis=0)
            n = ue.shape[0]
            tot, span = ue, 1
            while span < w:
                tot = tot + pltpu.roll(tot, span, 0)
                span *= 2
            if w // 2 - 1:
                tot = pltpu.roll(tot, n - (w // 2 - 1), 0)
            cnt = (jnp.minimum(t + w // 2, seq_len) - jnp.maximum(t - w // 2, 0)).astype(F32)
            dm = tot[hal:hal + ts] / cnt - ue[hal:hal + ts]
            y = jnp.dot(dm.astype(BF16), wpool_ref[g], preferred_element_type=F32)
            pooled.append((y * pscale_ref[:, cols]).astype(BF16))
        return jnp.concatenate(pooled, axis=1)

    starts = list(range(0, tm, ts))
    pooled = pool(starts[0])
    for s, r0 in enumerate(starts):
        rows = slice(r0, r0 + ts)
        acc = x_ref[0, rows, :] + jnp.dot(a_ref[0, rows, :], wout_ref[0:ATTN_WIDTH, :],
                                          preferred_element_type=F32)
        pooled_next = pool(starts[s + 1]) if s + 1 < len(starts) else None
        x1 = acc + jnp.dot(pooled, wout_ref[ATTN_WIDTH:, :], preferred_element_type=F32)
        x1_ref[0, rows, :] = x1
        inv = lax.rsqrt(jnp.mean(x1 * x1, axis=-1, keepdims=True) + EPS)
        h2_ref[0, rows, :] = (x1 * inv * gffn_ref[...]).astype(BF16)
        pooled = pooled_next


def _mix(a, u, x, w_pool, pool_scale, w_out, g_ffn, tm, ts):
    B, L, _ = x.shape
    nb = tm // POOL_HALO
    nblk = L // POOL_HALO
    return pl.pallas_call(
        functools.partial(_mix_kernel, seq_len=L, ts=ts),
        grid=(B, L // tm),
        in_specs=[
            pl.BlockSpec((1, tm, ATTN_WIDTH), lambda b, i: (b, i, 0)),
            pl.BlockSpec((1, tm, POOL_WIDTH), lambda b, i: (b, i, 0)),
            pl.BlockSpec((1, POOL_HALO, POOL_WIDTH),
                         lambda b, i: (b, jnp.maximum(i * nb - 1, 0), 0)),
            pl.BlockSpec((1, POOL_HALO, POOL_WIDTH),
                         lambda b, i: (b, jnp.minimum((i + 1) * nb, nblk - 1), 0)),
            pl.BlockSpec((1, tm, D_MODEL), lambda b, i: (b, i, 0)),
            pl.BlockSpec(w_pool.shape, lambda b, i: (0, 0, 0)),
            pl.BlockSpec((1, POOL_WIDTH), lambda b, i: (0, 0)),
            pl.BlockSpec(w_out.shape, lambda b, i: (0, 0)),
            pl.BlockSpec((1, D_MODEL), lambda b, i: (0, 0)),
        ],
        out_specs=[
            pl.BlockSpec((1, tm, D_MODEL), lambda b, i: (b, i, 0)),
            pl.BlockSpec((1, tm, D_MODEL), lambda b, i: (b, i, 0)),
        ],
        out_shape=[
            jax.ShapeDtypeStruct((B, L, D_MODEL), F32),
            jax.ShapeDtypeStruct((B, L, D_MODEL), BF16),
        ],
        compiler_params=_params("parallel", "parallel"),
        name="mix_out",
    )(a, u, u, u, x, w_pool, pool_scale, w_out, g_ffn)


def _ffn_kernel(h_ref, hprev_ref, hnext_ref, x1_hbm, wg_ref, wv_ref, wc_ref, bc_ref, wd_ref,
                o_ref, hext_ref, act_ref, x1_sem):
    b = pl.program_id(0)
    i = pl.program_id(1)
    j = pl.program_id(2)
    last_i = pl.num_programs(1) - 1
    nf = pl.num_programs(2) - 1
    tm = h_ref.shape[1]
    hal = CONV_HALO

    def x1_copy():
        row0 = pl.multiple_of(i * tm, tm)
        return pltpu.make_async_copy(x1_hbm.at[b, pl.ds(row0, tm), :], o_ref.at[0], x1_sem)

    def up():
        ge = jnp.dot(hext_ref[...], wg_ref[...], preferred_element_type=F32)
        val = jnp.dot(h_ref[0], wv_ref[...], preferred_element_type=F32)
        n = ge.shape[0]
        prev = pltpu.roll(ge, 1, 0)[hal:hal + tm]
        nxt = pltpu.roll(ge, n - 1, 0)[hal:hal + tm]
        tf = ge.shape[1]
        cols = pl.ds(pl.multiple_of(j * tf, tf), tf)
        gate = (prev * wc_ref[0:1, cols] + ge[hal:hal + tm] * wc_ref[1:2, cols]
                + nxt * wc_ref[2:3, cols] + bc_ref[:, cols])
        act_ref[...] = (gate * (1.0 / (1.0 + jnp.exp(-gate))) * val).astype(BF16)

    def down():
        o_ref[0] += jnp.dot(act_ref[...], wd_ref[...], preferred_element_type=F32)

    @pl.when(j == 0)
    def _():
        hext_ref[0:hal, :] = jnp.where(i > 0, hprev_ref[0], jnp.zeros_like(hprev_ref[0]))
        hext_ref[hal:hal + tm, :] = h_ref[0]
        hext_ref[hal + tm:, :] = jnp.where(i < last_i, hnext_ref[0], jnp.zeros_like(hnext_ref[0]))
        x1_copy().start()
        up()

    @pl.when(j == 1)
    def _():
        x1_copy().wait()

    @pl.when((j > 0) & (j < nf))
    def _():
        down()
        up()

    @pl.when(j == nf)
    def _():
        down()


def _ffn(h2, x1, w_up, w_conv, b_conv, w_down, tm, tf):
    B, L, _ = x1.shape
    nb = tm // CONV_HALO
    nblk = L // CONV_HALO
    nf = D_FF // tf
    up_j = lambda j: jnp.minimum(j, nf - 1)
    down_j = lambda j: jnp.maximum(j - 1, 0)
    return pl.pallas_call(
        _ffn_kernel,
        grid=(B, L // tm, nf + 1),
        in_specs=[
            pl.BlockSpec((1, tm, D_MODEL), lambda b, i, j: (b, i, 0)),
            pl.BlockSpec((1, CONV_HALO, D_MODEL),
                         lambda b, i, j: (b, jnp.maximum(i * nb - 1, 0), 0)),
            pl.BlockSpec((1, CONV_HALO, D_MODEL),
                         lambda b, i, j: (b, jnp.minimum((i + 1) * nb, nblk - 1), 0)),
            pl.BlockSpec(memory_space=pl.ANY),
            pl.BlockSpec((D_MODEL, tf), lambda b, i, j: (0, up_j(j))),
            pl.BlockSpec((D_MODEL, tf), lambda b, i, j: (0, up_j(j) + nf)),
            pl.BlockSpec(w_conv.shape, lambda b, i, j: (0, 0)),
            pl.BlockSpec(b_conv.shape, lambda b, i, j: (0, 0)),
            pl.BlockSpec((tf, D_MODEL), lambda b, i, j: (down_j(j), 0)),
        ],
        out_specs=pl.BlockSpec((1, tm, D_MODEL), lambda b, i, j: (b, i, 0)),
        out_shape=jax.ShapeDtypeStruct((B, L, D_MODEL), F32),
        scratch_shapes=[
            pltpu.VMEM((tm + 2 * CONV_HALO, D_MODEL), BF16),
            pltpu.VMEM((tm, tf), BF16),
            pltpu.SemaphoreType.DMA,
        ],
        compiler_params=_params("arbitrary", "arbitrary", "arbitrary"),
        name="ffn",
    )(h2, h2, h2, x1, w_up, w_up, w_conv, b_conv, w_down)


def _tiles(L):
    tm = min(512, L)
    tq = min(256, L)
    tk = min(512, L)
    tf = 512
    attn_unroll = 8
    tm_ffn = min(1024, L)
    ts = min(256, tm)
    return tm, tq, tk, tf, attn_unroll, tm_ffn, ts


def _layer(x, p):
    B, L, _ = x.shape
    tm, tq, tk, tf, attn_unroll, tm_ffn, ts = _tiles(L)
    tables = _rope_tables(L)
    qt, k, vt, u = _proj(x, p["g_mix"], p["w_in"], p["gq"], p["gk"], tables, tm, ts)
    a = _attention(qt, k, vt, tq, tk, attn_unroll)
    x1, h2 = _mix(a, u, x, p["w_pool"], p["pool_scale"], p["w_out"], p["g_ffn"], tm, ts)
    return _ffn(h2, x1, p["w_up"], p["w_conv"], p["b_conv"], p["w_down"], tm_ffn, tf)


def _prep(i, g_norm_mix, w_in, g_q, g_k, w_pool, pool_scale, w_out, g_norm_ffn, w_up, w_conv,
          b_conv, w_down):
    q_scale = HEAD_DIM ** -0.5 * math.log2(math.e)
    return dict(
        g_mix=g_norm_mix[i][None, :],
        w_in=w_in[i].astype(BF16),
        gq=(g_q[i] * q_scale)[None, :],
        gk=g_k[i][None, :],
        w_pool=w_pool[i].astype(BF16),
        pool_scale=pool_scale[i][None, :],
        w_out=w_out[i].astype(BF16),
        g_ffn=g_norm_ffn[i][None, :],
        w_up=w_up[i].astype(BF16),
        w_conv=w_conv[i],
        b_conv=b_conv[i][None, :],
        w_down=w_down[i].astype(BF16),
    )


def kernel(x_prompt, x_sample, g_norm_mix, w_in, g_q, g_k, w_pool, pool_scale, w_out,
           g_norm_ffn, w_up, w_conv, b_conv, w_down):
    depth = w_in.shape[0]
    layers = [_prep(i, g_norm_mix, w_in, g_q, g_k, w_pool, pool_scale, w_out, g_norm_ffn,
                    w_up, w_conv, b_conv, w_down) for i in range(depth)]
    outs = []
    for x in (x_prompt, x_sample):
        for p in layers:
            x = _layer(x, p)
        outs.append(x)
    return tuple(outs)
```

```python
import functools
import math

import jax
import jax.numpy as jnp
from jax import lax
from jax.experimental import pallas as pl
from jax.experimental.pallas import tpu as pltpu

D_MODEL = 2048
GRID_W = 64
HEAD_DIM = 128
N_Q_HEADS = 8
N_KV_HEADS = 2
Q_PER_KV = N_Q_HEADS // N_KV_HEADS
ATTN_WIDTH = N_Q_HEADS * HEAD_DIM
KV_WIDTH = N_KV_HEADS * HEAD_DIM
POOL_WINDOWS = (2, 4, 8, 16)
POOL_GROUP_WIDTH = 256
POOL_WIDTH = len(POOL_WINDOWS) * POOL_GROUP_WIDTH
MIX_WIDTH = ATTN_WIDTH + POOL_WIDTH
D_FF = 5632
ROPE_THETA = 10000.0
ROPE_AXIS_DIM = HEAD_DIM // 2
EPS = 1e-6

F32 = jnp.float32
BF16 = jnp.bfloat16

V7X_VMEM_BYTES = 64 * 1024 * 1024
SUBLANES_F32 = 8
SUBLANES_BF16 = 16
POOL_HALO = max(POOL_WINDOWS) // 2
assert POOL_HALO == SUBLANES_F32
CONV_HALO = SUBLANES_BF16

VMEM_LIMIT = 56 * 1024 * 1024


def _params(*sem):
    return pltpu.CompilerParams(dimension_semantics=sem, vmem_limit_bytes=VMEM_LIMIT)


def _rope_tables(L):
    n_rows = L // GRID_W
    half = ROPE_AXIS_DIM // 2
    inv_freq = ROPE_THETA ** (-jnp.arange(0, ROPE_AXIS_DIM, 2, dtype=F32) / ROPE_AXIS_DIM)
    ang_r = jnp.arange(n_rows, dtype=F32)[:, None] * inv_freq
    ang_c = jnp.arange(GRID_W, dtype=F32)[:, None] * inv_freq
    by_row = lambda v: jnp.broadcast_to(v[:, None, :], (n_rows, GRID_W, half))
    by_col = lambda v: jnp.broadcast_to(v[None, :, :], (n_rows, GRID_W, half))
    cos_r, sin_r = by_row(jnp.cos(ang_r)), by_row(jnp.sin(ang_r))
    cos_c, sin_c = by_col(jnp.cos(ang_c)), by_col(jnp.sin(ang_c))
    zero = jnp.zeros((n_rows, GRID_W, half), F32)
    table = lambda parts: jnp.concatenate(parts, axis=-1).reshape(L, HEAD_DIM)
    cos = table([cos_r, cos_r, cos_c, cos_c])
    sin_up = table([-sin_r, zero, -sin_c, zero])
    sin_dn = table([zero, sin_r, zero, sin_c])
    return cos, sin_up, sin_dn


def _proj_kernel(x_ref, gmix_ref, win_ref, gq_ref, gk_ref, cos_ref, sup_ref, sdn_ref, *rest,
                 ts, n_cast):
    cast_in, (qt_ref, k_ref, vt_ref, u_ref), cast_out = (
        rest[:n_cast], rest[n_cast:n_cast + 4], rest[n_cast + 4:])
    for src, dst in zip(cast_in, cast_out):
        dst[...] = src[...].astype(BF16)
    tm = x_ref.shape[1]
    for r0 in range(0, tm, ts):
        rows = slice(r0, r0 + ts)
        x = x_ref[0, rows, :]
        inv = lax.rsqrt(jnp.mean(x * x, axis=-1, keepdims=True) + EPS)
        h = (x * inv * gmix_ref[...]).astype(BF16)
        cos, sup, sdn = cos_ref[rows, :], sup_ref[rows, :], sdn_ref[rows, :]

        def norm_rope(zh, g):
            r = lax.rsqrt(jnp.mean(zh * zh, axis=-1, keepdims=True) + EPS)
            n = zh * r * g
            up = pltpu.roll(n, HEAD_DIM - ROPE_AXIS_DIM // 2, 1)
            dn = pltpu.roll(n, ROPE_AXIS_DIM // 2, 1)
            return n * cos + up * sup + dn * sdn

        zq = jnp.dot(h, win_ref[:, 0:ATTN_WIDTH], preferred_element_type=F32)
        for hh in range(N_Q_HEADS):
            sl = slice(hh * HEAD_DIM, (hh + 1) * HEAD_DIM)
            qt_ref[0, sl, rows] = norm_rope(zq[:, sl], gq_ref[...]).T.astype(BF16)
        zk = jnp.dot(h, win_ref[:, ATTN_WIDTH:ATTN_WIDTH + KV_WIDTH], preferred_element_type=F32)
        for hh in range(N_KV_HEADS):
            sl = slice(hh * HEAD_DIM, (hh + 1) * HEAD_DIM)
            k_ref[0, rows, sl] = norm_rope(zk[:, sl], gk_ref[...]).astype(BF16)
        zv = jnp.dot(h, win_ref[:, ATTN_WIDTH + KV_WIDTH:ATTN_WIDTH + 2 * KV_WIDTH],
                     preferred_element_type=F32)
        for hh in range(N_KV_HEADS):
            sl = slice(hh * HEAD_DIM, (hh + 1) * HEAD_DIM)
            vt_ref[0, sl, rows] = zv[:, sl].T.astype(BF16)
        u_ref[0, rows, :] = jnp.dot(h, win_ref[:, ATTN_WIDTH + 2 * KV_WIDTH:],
                                    preferred_element_type=F32)


def _proj(x, g_mix, w_in, gq, gk, tables, tm, ts, cast=()):
    B, L, _ = x.shape
    nt = L // tm
    cos, sup, sdn = tables
    tab_spec = pl.BlockSpec((tm, HEAD_DIM), lambda b, i: (i, 0))
    vec = lambda n: pl.BlockSpec((1, n), lambda b, i: (0, 0))
    slab_specs = []
    for w in cast:
        rows, cols = w.shape
        assert rows % (B * nt * SUBLANES_BF16) == 0
        slab_specs.append(pl.BlockSpec((rows // (B * nt), cols), lambda b, i: (b * nt + i, 0)))
    return pl.pallas_call(
        functools.partial(_proj_kernel, ts=ts, n_cast=len(cast)),
        grid=(B, nt),
        in_specs=[
            pl.BlockSpec((1, tm, D_MODEL), lambda b, i: (b, i, 0)),
            vec(D_MODEL),
            pl.BlockSpec(w_in.shape, lambda b, i: (0, 0)),
            vec(HEAD_DIM), vec(HEAD_DIM),
            tab_spec, tab_spec, tab_spec,
            *slab_specs,
        ],
        out_specs=[
            pl.BlockSpec((1, ATTN_WIDTH, tm), lambda b, i: (b, 0, i)),
            pl.BlockSpec((1, tm, KV_WIDTH), lambda b, i: (b, i, 0)),
            pl.BlockSpec((1, KV_WIDTH, tm), lambda b, i: (b, 0, i)),
            pl.BlockSpec((1, tm, POOL_WIDTH), lambda b, i: (b, i, 0)),
            *slab_specs,
        ],
        out_shape=[
            jax.ShapeDtypeStruct((B, ATTN_WIDTH, L), BF16),
            jax.ShapeDtypeStruct((B, L, KV_WIDTH), BF16),
            jax.ShapeDtypeStruct((B, KV_WIDTH, L), BF16),
            jax.ShapeDtypeStruct((B, L, POOL_WIDTH), F32),
            *[jax.ShapeDtypeStruct(w.shape, BF16) for w in cast],
        ],
        compiler_params=_params("parallel", "parallel"),
        name="proj",
    )(x, g_mix, w_in, gq, gk, cos, sup, sdn, *cast)


def _attn_kernel(qt_ref, k_ref, vt_ref, o_ref, acc_ref, s_ref, *, tq, tk, unroll):
    L = k_ref.shape[1]
    n = Q_PER_KV * tq
    nk = L // tk
    nq = L // tq
    assert nk % unroll == 0 and unroll % 2 == 0
    ones_rows = (lax.broadcasted_iota(jnp.int32, (SUBLANES_BF16, tk), 0) == 0).astype(BF16)

    def qcat_at(qi):
        c0 = pl.multiple_of(qi * tq, tq)
        return jnp.concatenate(
            [qt_ref[0, g * HEAD_DIM:(g + 1) * HEAD_DIM, pl.ds(c0, tq)] for g in range(Q_PER_KV)],
            axis=1)

    def scores(qc, j):
        start = pl.multiple_of(j * tk, tk)
        return jnp.dot(k_ref[0, pl.ds(start, tk), :], qc,
                       preferred_element_type=F32)

    def issue_scores(slot, qc, j):
        s = scores(qc, j)
        s_ref[slot] = s
        return jnp.max(s, axis=0, keepdims=True)

    def step(j, slot, aslot, m, cmax, next_q, next_j):
        cmax_next = issue_scores(1 - slot, next_q, next_j)
        m_new = jnp.maximum(m, cmax)
        alpha = jnp.exp2(m - m_new)
        pt = jnp.exp2(s_ref[slot] - m_new).astype(BF16)
        start = pl.multiple_of(j * tk, tk)
        vc = jnp.concatenate([vt_ref[0, :, pl.ds(start, tk)], ones_rows], axis=0)
        acc_ref[aslot] = alpha * acc_ref[aslot] + jnp.dot(vc, pt, preferred_element_type=F32)
        return m_new, cmax_next

    def finish(qi, aslot):
        out_t = acc_ref[aslot, 0:HEAD_DIM, :] / acc_ref[aslot, HEAD_DIM:HEAD_DIM + 1, :]
        r0 = pl.multiple_of(qi * tq, tq)
        for g in range(Q_PER_KV):
            o_ref[0, pl.ds(r0, tq), g * HEAD_DIM:(g + 1) * HEAD_DIM] = (
                out_t[:, g * tq:(g + 1) * tq].T.astype(BF16))

    def tile(qi, cmax):
        qc = qcat_at(qi)
        aslot = lax.rem(qi, 2)
        acc_ref[aslot] = jnp.zeros(acc_ref.shape[1:], F32)

        def group(jj, carry):
            for si in range(unroll):
                j = jj * unroll + si
                carry = step(j, si % 2, aslot, *carry, qc, j + 1)
            return carry

        carry = lax.fori_loop(0, nk // unroll - 1, group,
                              (jnp.full((1, n), -jnp.inf, F32), cmax))
        for si in range(unroll - 1):
            j = nk - unroll + si
            carry = step(j, si % 2, aslot, *carry, qc, j + 1)
            if si == 0:
                finish(jnp.maximum(qi - 1, 0), 1 - aslot)
        _, cmax = step(nk - 1, 1, aslot, *carry, qcat_at(jnp.minimum(qi + 1, nq - 1)), 0)
        return cmax

    acc_ref[1] = jnp.ones(acc_ref.shape[1:], F32)
    lax.fori_loop(0, nq, tile, issue_scores(0, qcat_at(0), 0))
    finish(nq - 1, (nq - 1) % 2)


def _attention(qt, k, vt, tq, tk, unroll):
    B, _, L = qt.shape
    gw = Q_PER_KV * HEAD_DIM
    return pl.pallas_call(
        functools.partial(_attn_kernel, tq=tq, tk=tk, unroll=unroll),
        grid=(B, N_KV_HEADS),
        in_specs=[
            pl.BlockSpec((1, gw, L), lambda b, kh: (b, kh, 0)),
            pl.BlockSpec((1, L, HEAD_DIM), lambda b, kh: (b, 0, kh)),
            pl.BlockSpec((1, HEAD_DIM, L), lambda b, kh: (b, kh, 0)),
        ],
        out_specs=pl.BlockSpec((1, L, gw), lambda b, kh: (b, 0, kh)),
        out_shape=jax.ShapeDtypeStruct((B, L, ATTN_WIDTH), BF16),
        scratch_shapes=[
            pltpu.VMEM((2, HEAD_DIM + SUBLANES_BF16, Q_PER_KV * tq), F32),
            pltpu.VMEM((2, tk, Q_PER_KV * tq), F32),
        ],
        compiler_params=_params("parallel", "parallel"),
        name="attention",
    )(qt, k, vt)


def _mix_kernel(a_ref, u_ref, uprev_ref, unext_ref, x_ref, wpool_ref, pscale_ref, wout_ref,
                gffn_ref, x1_ref, h2_ref, *, seq_len, ts):
    i = pl.program_id(1)
    last = pl.num_programs(1) - 1
    tm = u_ref.shape[1]
    hal = POOL_HALO

    def pool(r0):
        rows = slice(r0, r0 + ts)
        t = i * tm + r0 + lax.broadcasted_iota(jnp.int32, (ts, 1), 0)
        pooled = []
        for g, w in enumerate(POOL_WINDOWS):
            c0 = g * POOL_GROUP_WIDTH
            cols = slice(c0, c0 + POOL_GROUP_WIDTH)
            before = (u_ref[0, r0 - hal:r0, cols] if r0 > 0
                      else jnp.where(i > 0, uprev_ref[0, :, cols], 0.0))
            after = (u_ref[0, r0 + ts:r0 + ts + hal, cols] if r0 + ts < tm
                     else jnp.where(i < last, unext_ref[0, :, cols], 0.0))
            ue = jnp.concatenate([before, u_ref[0, rows, cols], after], axis=0)
            n = ue.shape[0]
            tot, span = ue, 1
            while span < w:
                tot = tot + pltpu.roll(tot, span, 0)
                span *= 2
            if w // 2 - 1:
                tot = pltpu.roll(tot, n - (w // 2 - 1), 0)
            cnt = (jnp.minimum(t + w // 2, seq_len) - jnp.maximum(t - w // 2, 0)).astype(F32)
            dm = tot[hal:hal + ts] / cnt - ue[hal:hal + ts]
            y = jnp.dot(dm.astype(BF16), wpool_ref[g], preferred_element_type=F32)
            pooled.append((y * pscale_ref[:, cols]).astype(BF16))
        return jnp.concatenate(pooled, axis=1)

    starts = list(range(0, tm, ts))
    pooled = pool(starts[0])
    for s, r0 in enumerate(starts):
        rows = slice(r0, r0 + ts)
        acc = x_ref[0, rows, :] + jnp.dot(a_ref[0, rows, :], wout_ref[0:ATTN_WIDTH, :],
                                          preferred_element_type=F32)
        pooled_next = pool(starts[s + 1]) if s + 1 < len(starts) else None
        x1 = acc + jnp.dot(pooled, wout_ref[ATTN_WIDTH:, :], preferred_element_type=F32)
        x1_ref[0, rows, :] = x1
        inv = lax.rsqrt(jnp.mean(x1 * x1, axis=-1, keepdims=True) + EPS)
        h2_ref[0, rows, :] = (x1 * inv * gffn_ref[...]).astype(BF16)
        pooled = pooled_next


def _mix(a, u, x, w_pool, pool_scale, w_out, g_ffn, tm, ts):
    B, L, _ = x.shape
    nb = tm // POOL_HALO
    nblk = L // POOL_HALO
    return pl.pallas_call(
        functools.partial(_mix_kernel, seq_len=L, ts=ts),
        grid=(B, L // tm),
        in_specs=[
            pl.BlockSpec((1, tm, ATTN_WIDTH), lambda b, i: (b, i, 0)),
            pl.BlockSpec((1, tm, POOL_WIDTH), lambda b, i: (b, i, 0)),
            pl.BlockSpec((1, POOL_HALO, POOL_WIDTH),
                         lambda b, i: (b, jnp.maximum(i * nb - 1, 0), 0)),
            pl.BlockSpec((1, POOL_HALO, POOL_WIDTH),
                         lambda b, i: (b, jnp.minimum((i + 1) * nb, nblk - 1), 0)),
            pl.BlockSpec((1, tm, D_MODEL), lambda b, i: (b, i, 0)),
            pl.BlockSpec(w_pool.shape, lambda b, i: (0, 0, 0)),
            pl.BlockSpec((1, POOL_WIDTH), lambda b, i: (0, 0)),
            pl.BlockSpec(w_out.shape, lambda b, i: (0, 0)),
            pl.BlockSpec((1, D_MODEL), lambda b, i: (0, 0)),
        ],
        out_specs=[
            pl.BlockSpec((1, tm, D_MODEL), lambda b, i: (b, i, 0)),
            pl.BlockSpec((1, tm, D_MODEL), lambda b, i: (b, i, 0)),
        ],
        out_shape=[
            jax.ShapeDtypeStruct((B, L, D_MODEL), F32),
            jax.ShapeDtypeStruct((B, L, D_MODEL), BF16),
        ],
        compiler_params=_params("parallel", "parallel"),
        name="mix_out",
    )(a, u, u, u, x, w_pool, pool_scale, w_out, g_ffn)


def _ffn_kernel(h_ref, hprev_ref, hnext_ref, x1_hbm, wg_ref, wv_ref, wc_ref, bc_ref, wd_ref,
                o_ref, hext_ref, act_ref, x1_sem):
    b = pl.program_id(0)
    i = pl.program_id(1)
    j = pl.program_id(2)
    last_i = pl.num_programs(1) - 1
    nf = pl.num_programs(2) - 1
    tm = h_ref.shape[1]
    hal = CONV_HALO

    def x1_copy():
        row0 = pl.multiple_of(i * tm, tm)
        return pltpu.make_async_copy(x1_hbm.at[b, pl.ds(row0, tm), :], o_ref.at[0], x1_sem)

    def up():
        ge = jnp.dot(hext_ref[...], wg_ref[...], preferred_element_type=F32)
        val = jnp.dot(h_ref[0], wv_ref[...], preferred_element_type=F32)
        n = ge.shape[0]
        prev = pltpu.roll(ge, 1, 0)[hal:hal + tm]
        nxt = pltpu.roll(ge, n - 1, 0)[hal:hal + tm]
        tf = ge.shape[1]
        cols = pl.ds(pl.multiple_of(j * tf, tf), tf)
        gate = (prev * wc_ref[0:1, cols] + ge[hal:hal + tm] * wc_ref[1:2, cols]
                + nxt * wc_ref[2:3, cols] + bc_ref[:, cols])
        act_ref[...] = (gate * (1.0 / (1.0 + jnp.exp(-gate))) * val).astype(BF16)

    def down():
        o_ref[0] += jnp.dot(act_ref[...], wd_ref[...], preferred_element_type=F32)

    @pl.when(j == 0)
    def _():
        hext_ref[0:hal, :] = jnp.where(i > 0, hprev_ref[0], jnp.zeros_like(hprev_ref[0]))
        hext_ref[hal:hal + tm, :] = h_ref[0]
        hext_ref[hal + tm:, :] = jnp.where(i < last_i, hnext_ref[0], jnp.zeros_like(hnext_ref[0]))
        x1_copy().start()
        up()

    @pl.when(j == 1)
    def _():
        x1_copy().wait()

    @pl.when((j > 0) & (j < nf))
    def _():
        down()
        up()

    @pl.when(j == nf)
    def _():
        down()


def _ffn(h2, x1, w_up, w_conv, b_conv, w_down, tm, tf):
    B, L, _ = x1.shape
    nb = tm // CONV_HALO
    nblk = L // CONV_HALO
    nf = D_FF // tf
    up_j = lambda j: jnp.minimum(j, nf - 1)
    down_j = lambda j: jnp.maximum(j - 1, 0)
    return pl.pallas_call(
        _ffn_kernel,
        grid=(B, L // tm, nf + 1),
        in_specs=[
            pl.BlockSpec((1, tm, D_MODEL), lambda b, i, j: (b, i, 0)),
            pl.BlockSpec((1, CONV_HALO, D_MODEL),
                         lambda b, i, j: (b, jnp.maximum(i * nb - 1, 0), 0)),
            pl.BlockSpec((1, CONV_HALO, D_MODEL),
                         lambda b, i, j: (b, jnp.minimum((i + 1) * nb, nblk - 1), 0)),
            pl.BlockSpec(memory_space=pl.ANY),
            pl.BlockSpec((D_MODEL, tf), lambda b, i, j: (0, up_j(j))),
            pl.BlockSpec((D_MODEL, tf), lambda b, i, j: (0, up_j(j) + nf)),
            pl.BlockSpec(w_conv.shape, lambda b, i, j: (0, 0)),
            pl.BlockSpec(b_conv.shape, lambda b, i, j: (0, 0)),
            pl.BlockSpec((tf, D_MODEL), lambda b, i, j: (down_j(j), 0)),
        ],
        out_specs=pl.BlockSpec((1, tm, D_MODEL), lambda b, i, j: (b, i, 0)),
        out_shape=jax.ShapeDtypeStruct((B, L, D_MODEL), F32),
        scratch_shapes=[
            pltpu.VMEM((tm + 2 * CONV_HALO, D_MODEL), BF16),
            pltpu.VMEM((tm, tf), BF16),
            pltpu.SemaphoreType.DMA,
        ],
        compiler_params=_params("arbitrary", "arbitrary", "arbitrary"),
        name="ffn",
    )(h2, h2, h2, x1, w_up, w_up, w_conv, b_conv, w_down)


def _tiles(L):
    tm = min(512, L)
    tq = min(256, L)
    tk = min(512, L)
    tf = 512
    attn_unroll = 8
    tm_ffn = min(1024, L)
    ts = min(256, tm)
    return tm, tq, tk, tf, attn_unroll, tm_ffn, ts


LATE_WEIGHTS = ("w_out", "w_up", "w_down")


def _layer(x, p):
    B, L, _ = x.shape
    tm, tq, tk, tf, attn_unroll, tm_ffn, ts = _tiles(L)
    tables = _rope_tables(L)
    pending = [n for n in LATE_WEIGHTS if p[n].dtype != BF16]
    qt, k, vt, u, *cast = _proj(x, p["g_mix"], p["w_in"], p["gq"], p["gk"], tables, tm, ts,
                                cast=[p[n] for n in pending])
    p = {**p, **dict(zip(pending, cast))}
    a = _attention(qt, k, vt, tq, tk, attn_unroll)
    x1, h2 = _mix(a, u, x, p["w_pool"], p["pool_scale"], p["w_out"], p["g_ffn"], tm, ts)
    return _ffn(h2, x1, p["w_up"], p["w_conv"], p["b_conv"], p["w_down"], tm_ffn, tf), p


def _prep(i, g_norm_mix, w_in, g_q, g_k, w_pool, pool_scale, w_out, g_norm_ffn, w_up, w_conv,
          b_conv, w_down):
    q_scale = HEAD_DIM ** -0.5 * math.log2(math.e)
    return dict(
        g_mix=g_norm_mix[i][None, :],
        w_in=w_in[i].astype(BF16),
        gq=(g_q[i] * q_scale)[None, :],
        gk=g_k[i][None, :],
        w_pool=w_pool[i].astype(BF16),
        pool_scale=pool_scale[i][None, :],
        w_out=w_out[i],
        g_ffn=g_norm_ffn[i][None, :],
        w_up=w_up[i],
        w_conv=w_conv[i],
        b_conv=b_conv[i][None, :],
        w_down=w_down[i],
    )


def kernel(x_prompt, x_sample, g_norm_mix, w_in, g_q, g_k, w_pool, pool_scale, w_out,
           g_norm_ffn, w_up, w_conv, b_conv, w_down):
    depth = w_in.shape[0]
    xs = [x_prompt, x_sample]
    for i in range(depth):
        p = _prep(i, g_norm_mix, w_in, g_q, g_k, w_pool, pool_scale, w_out, g_norm_ffn,
                  w_up, w_conv, b_conv, w_down)
        for g in range(len(xs)):
            xs[g], p = _layer(xs[g], p)
    return tuple(xs)
```

```python
import functools
import math

import jax
import jax.numpy as jnp
from jax import lax
from jax.experimental import pallas as pl
from jax.experimental.pallas import tpu as pltpu

D_MODEL = 2048
GRID_W = 64
HEAD_DIM = 128
N_Q_HEADS = 8
N_KV_HEADS = 2
Q_PER_KV = N_Q_HEADS // N_KV_HEADS
ATTN_WIDTH = N_Q_HEADS * HEAD_DIM
KV_WIDTH = N_KV_HEADS * HEAD_DIM
POOL_WINDOWS = (2, 4, 8, 16)
POOL_GROUP_WIDTH = 256
POOL_WIDTH = len(POOL_WINDOWS) * POOL_GROUP_WIDTH
MIX_WIDTH = ATTN_WIDTH + POOL_WIDTH
D_FF = 5632
ROPE_THETA = 10000.0
ROPE_AXIS_DIM = HEAD_DIM // 2
EPS = 1e-6

F32 = jnp.float32
BF16 = jnp.bfloat16

V7X_VMEM_BYTES = 64 * 1024 * 1024
SUBLANES_F32 = 8
SUBLANES_BF16 = 16
POOL_HALO = max(POOL_WINDOWS) // 2
assert POOL_HALO == SUBLANES_F32
CONV_HALO = SUBLANES_BF16

VMEM_LIMIT = 56 * 1024 * 1024


def _params(*sem):
    return pltpu.CompilerParams(dimension_semantics=sem, vmem_limit_bytes=VMEM_LIMIT)


def _rope_tables(L):
    half = ROPE_AXIS_DIM // 2
    inv_freq = ROPE_THETA ** (-jnp.arange(0, ROPE_AXIS_DIM, 2, dtype=F32) / ROPE_AXIS_DIM)

    def axis_tables(n):
        ang = jnp.arange(n, dtype=F32)[:, None] * inv_freq
        cos, sin, zero = jnp.cos(ang), jnp.sin(ang), jnp.zeros((n, half), F32)
        return [jnp.concatenate(t, axis=1) for t in ((cos, cos), (-sin, zero), (zero, sin))]

    by_row = jnp.stack([jnp.pad(t, ((0, 0), (0, ROPE_AXIS_DIM))) for t in axis_tables(L // GRID_W)])
    by_col = jnp.stack([jnp.pad(t, ((0, 0), (ROPE_AXIS_DIM, 0))) for t in axis_tables(GRID_W)])
    return by_row, by_col


def _proj_kernel(x_ref, gmix_ref, win_ref, gq_ref, gk_ref, rrow_ref, rcol_ref, *rest,
                 ts, n_cast):
    cast_in, (qt_ref, k_ref, vt_ref, u_ref), cast_out = (
        rest[:n_cast], rest[n_cast:n_cast + 4], rest[n_cast + 4:])
    for src, dst in zip(cast_in, cast_out):
        dst[...] = src[...].astype(BF16)
    tm = x_ref.shape[1]
    for r0 in range(0, tm, ts):
        rows = slice(r0, r0 + ts)
        x = x_ref[0, rows, :]
        inv = lax.rsqrt(jnp.mean(x * x, axis=-1, keepdims=True) + EPS)
        h = (x * inv * gmix_ref[...]).astype(BF16)
        row0 = (pl.program_id(1) * tm + r0) // GRID_W

        def rope_table(c):
            by_row = jnp.concatenate(
                [jnp.broadcast_to(rrow_ref[c, pl.ds(row0 + q, 1), :], (GRID_W, HEAD_DIM))
                 for q in range(ts // GRID_W)], axis=0)
            return by_row + jnp.concatenate([rcol_ref[c]] * (ts // GRID_W), axis=0)

        cos, sup, sdn = rope_table(0), rope_table(1), rope_table(2)

        def norm_rope(zh, g):
            r = lax.rsqrt(jnp.mean(zh * zh, axis=-1, keepdims=True) + EPS)
            n = zh * r * g
            up = pltpu.roll(n, HEAD_DIM - ROPE_AXIS_DIM // 2, 1)
            dn = pltpu.roll(n, ROPE_AXIS_DIM // 2, 1)
            return n * cos + up * sup + dn * sdn

        zq = jnp.dot(h, win_ref[:, 0:ATTN_WIDTH], preferred_element_type=F32)
        for hh in range(N_Q_HEADS):
            sl = slice(hh * HEAD_DIM, (hh + 1) * HEAD_DIM)
            qt_ref[0, sl, rows] = norm_rope(zq[:, sl], gq_ref[...]).T.astype(BF16)
        zk = jnp.dot(h, win_ref[:, ATTN_WIDTH:ATTN_WIDTH + KV_WIDTH], preferred_element_type=F32)
        for hh in range(N_KV_HEADS):
            sl = slice(hh * HEAD_DIM, (hh + 1) * HEAD_DIM)
            k_ref[0, rows, sl] = norm_rope(zk[:, sl], gk_ref[...]).astype(BF16)
        zv = jnp.dot(h, win_ref[:, ATTN_WIDTH + KV_WIDTH:ATTN_WIDTH + 2 * KV_WIDTH],
                     preferred_element_type=F32)
        for hh in range(N_KV_HEADS):
            sl = slice(hh * HEAD_DIM, (hh + 1) * HEAD_DIM)
            vt_ref[0, sl, rows] = zv[:, sl].T.astype(BF16)
        u_ref[0, rows, :] = jnp.dot(h, win_ref[:, ATTN_WIDTH + 2 * KV_WIDTH:],
                                    preferred_element_type=F32)


def _proj(x, g_mix, w_in, gq, gk, tables, tm, ts, cast=()):
    B, L, _ = x.shape
    nt = L // tm
    rope_by_row, rope_by_col = tables
    assert ts % GRID_W == 0
    whole3 = lambda a: pl.BlockSpec(a.shape, lambda b, i: (0, 0, 0))
    vec = lambda n: pl.BlockSpec((1, n), lambda b, i: (0, 0))
    slab_specs = []
    for w in cast:
        rows, cols = w.shape
        assert rows % (B * nt * SUBLANES_BF16) == 0
        slab_specs.append(pl.BlockSpec((rows // (B * nt), cols), lambda b, i: (b * nt + i, 0)))
    return pl.pallas_call(
        functools.partial(_proj_kernel, ts=ts, n_cast=len(cast)),
        grid=(B, nt),
        in_specs=[
            pl.BlockSpec((1, tm, D_MODEL), lambda b, i: (b, i, 0)),
            vec(D_MODEL),
            pl.BlockSpec(w_in.shape, lambda b, i: (0, 0)),
            vec(HEAD_DIM), vec(HEAD_DIM),
            whole3(rope_by_row), whole3(rope_by_col),
            *slab_specs,
        ],
        out_specs=[
            pl.BlockSpec((1, ATTN_WIDTH, tm), lambda b, i: (b, 0, i)),
            pl.BlockSpec((1, tm, KV_WIDTH), lambda b, i: (b, i, 0)),
            pl.BlockSpec((1, KV_WIDTH, tm), lambda b, i: (b, 0, i)),
            pl.BlockSpec((1, tm, POOL_WIDTH), lambda b, i: (b, i, 0)),
            *slab_specs,
        ],
        out_shape=[
            jax.ShapeDtypeStruct((B, ATTN_WIDTH, L), BF16),
            jax.ShapeDtypeStruct((B, L, KV_WIDTH), BF16),
            jax.ShapeDtypeStruct((B, KV_WIDTH, L), BF16),
            jax.ShapeDtypeStruct((B, L, POOL_WIDTH), F32),
            *[jax.ShapeDtypeStruct(w.shape, BF16) for w in cast],
        ],
        compiler_params=_params("parallel", "parallel"),
        name="proj",
    )(x, g_mix, w_in, gq, gk, rope_by_row, rope_by_col, *cast)


def _attn_kernel(qt_ref, k_ref, vt_ref, o_ref, acc_ref, s_ref, *, tq, tk, unroll):
    L = k_ref.shape[1]
    n = Q_PER_KV * tq
    nk = L // tk
    nq = L // tq
    assert nk % unroll == 0 and unroll % 2 == 0
    ones_rows = (lax.broadcasted_iota(jnp.int32, (SUBLANES_BF16, tk), 0) == 0).astype(BF16)

    def qcat_at(qi):
        c0 = pl.multiple_of(qi * tq, tq)
        return jnp.concatenate(
            [qt_ref[0, g * HEAD_DIM:(g + 1) * HEAD_DIM, pl.ds(c0, tq)] for g in range(Q_PER_KV)],
            axis=1)

    def scores(qc, j):
        start = pl.multiple_of(j * tk, tk)
        return jnp.dot(k_ref[0, pl.ds(start, tk), :], qc,
                       preferred_element_type=F32)

    def issue_scores(slot, qc, j):
        s = scores(qc, j)
        s_ref[slot] = s
        return jnp.max(s, axis=0, keepdims=True)

    def step(j, slot, aslot, m, cmax, next_q, next_j):
        cmax_next = issue_scores(1 - slot, next_q, next_j)
        m_new = jnp.maximum(m, cmax)
        alpha = jnp.exp2(m - m_new)
        pt = jnp.exp2(s_ref[slot] - m_new).astype(BF16)
        start = pl.multiple_of(j * tk, tk)
        vc = jnp.concatenate([vt_ref[0, :, pl.ds(start, tk)], ones_rows], axis=0)
        acc_ref[aslot] = alpha * acc_ref[aslot] + jnp.dot(vc, pt, preferred_element_type=F32)
        return m_new, cmax_next

    def finish(qi, aslot):
        out_t = acc_ref[aslot, 0:HEAD_DIM, :] / acc_ref[aslot, HEAD_DIM:HEAD_DIM + 1, :]
        r0 = pl.multiple_of(qi * tq, tq)
        for g in range(Q_PER_KV):
            o_ref[0, pl.ds(r0, tq), g * HEAD_DIM:(g + 1) * HEAD_DIM] = (
                out_t[:, g * tq:(g + 1) * tq].T.astype(BF16))

    def tile(qi, cmax):
        qc = qcat_at(qi)
        aslot = lax.rem(qi, 2)
        acc_ref[aslot] = jnp.zeros(acc_ref.shape[1:], F32)

        def group(jj, carry):
            for si in range(unroll):
                j = jj * unroll + si
                carry = step(j, si % 2, aslot, *carry, qc, j + 1)
            return carry

        carry = lax.fori_loop(0, nk // unroll - 1, group,
                              (jnp.full((1, n), -jnp.inf, F32), cmax))
        for si in range(unroll - 1):
            j = nk - unroll + si
            carry = step(j, si % 2, aslot, *carry, qc, j + 1)
            if si == 0:
                finish(jnp.maximum(qi - 1, 0), 1 - aslot)
        _, cmax = step(nk - 1, 1, aslot, *carry, qcat_at(jnp.minimum(qi + 1, nq - 1)), 0)
        return cmax

    acc_ref[1] = jnp.ones(acc_ref.shape[1:], F32)
    lax.fori_loop(0, nq, tile, issue_scores(0, qcat_at(0), 0))
    finish(nq - 1, (nq - 1) % 2)


def _attention(qt, k, vt, tq, tk, unroll):
    B, _, L = qt.shape
    gw = Q_PER_KV * HEAD_DIM
    return pl.pallas_call(
        functools.partial(_attn_kernel, tq=tq, tk=tk, unroll=unroll),
        grid=(B, N_KV_HEADS),
        in_specs=[
            pl.BlockSpec((1, gw, L), lambda b, kh: (b, kh, 0)),
            pl.BlockSpec((1, L, HEAD_DIM), lambda b, kh: (b, 0, kh)),
            pl.BlockSpec((1, HEAD_DIM, L), lambda b, kh: (b, kh, 0)),
        ],
        out_specs=pl.BlockSpec((1, L, gw), lambda b, kh: (b, 0, kh)),
        out_shape=jax.ShapeDtypeStruct((B, L, ATTN_WIDTH), BF16),
        scratch_shapes=[
            pltpu.VMEM((2, HEAD_DIM + SUBLANES_BF16, Q_PER_KV * tq), F32),
            pltpu.VMEM((2, tk, Q_PER_KV * tq), F32),
        ],
        compiler_params=_params("parallel", "parallel"),
        name="attention",
    )(qt, k, vt)


def _mix_kernel(a_ref, u_ref, uprev_ref, unext_ref, x_ref, wpool_ref, pscale_ref, wout_ref,
                gffn_ref, x1_ref, h2_ref, *, seq_len, ts):
    i = pl.program_id(1)
    last = pl.num_programs(1) - 1
    tm = u_ref.shape[1]
    hal = POOL_HALO

    def pool(r0):
        rows = slice(r0, r0 + ts)
        t = i * tm + r0 + lax.broadcasted_iota(jnp.int32, (ts, 1), 0)
        pooled = []
        for g, w in enumerate(POOL_WINDOWS):
            c0 = g * POOL_GROUP_WIDTH
            cols = slice(c0, c0 + POOL_GROUP_WIDTH)
            before = (u_ref[0, r0 - hal:r0, cols] if r0 > 0
                      else jnp.where(i > 0, uprev_ref[0, :, cols], 0.0))
            after = (u_ref[0, r0 + ts:r0 + ts + hal, cols] if r0 + ts < tm
                     else jnp.where(i < last, unext_ref[0, :, cols], 0.0))
            ue = jnp.concatenate([before, u_ref[0, rows, cols], after], axis=0)
            n = ue.shape[0]
            tot, span = ue, 1
            while span < w:
                tot = tot + pltpu.roll(tot, span, 0)
                span *= 2
            if w // 2 - 1:
                tot = pltpu.roll(tot, n - (w // 2 - 1), 0)
            cnt = (jnp.minimum(t + w // 2, seq_len) - jnp.maximum(t - w // 2, 0)).astype(F32)
            dm = tot[hal:hal + ts] / cnt - ue[hal:hal + ts]
            y = jnp.dot(dm.astype(BF16), wpool_ref[g], preferred_element_type=F32)
            pooled.append((y * pscale_ref[:, cols]).astype(BF16))
        return jnp.concatenate(pooled, axis=1)

    starts = list(range(0, tm, ts))
    pooled = pool(starts[0])
    for s, r0 in enumerate(starts):
        rows = slice(r0, r0 + ts)
        acc = x_ref[0, rows, :] + jnp.dot(a_ref[0, rows, :], wout_ref[0:ATTN_WIDTH, :],
                                          preferred_element_type=F32)
        pooled_next = pool(starts[s + 1]) if s + 1 < len(starts) else None
        x1 = acc + jnp.dot(pooled, wout_ref[ATTN_WIDTH:, :], preferred_element_type=F32)
        x1_ref[0, rows, :] = x1
        inv = lax.rsqrt(jnp.mean(x1 * x1, axis=-1, keepdims=True) + EPS)
        h2_ref[0, rows, :] = (x1 * inv * gffn_ref[...]).astype(BF16)
        pooled = pooled_next


def _mix(a, u, x, w_pool, pool_scale, w_out, g_ffn, tm, ts):
    B, L, _ = x.shape
    nb = tm // POOL_HALO
    nblk = L // POOL_HALO
    return pl.pallas_call(
        functools.partial(_mix_kernel, seq_len=L, ts=ts),
        grid=(B, L // tm),
        in_specs=[
            pl.BlockSpec((1, tm, ATTN_WIDTH), lambda b, i: (b, i, 0)),
            pl.BlockSpec((1, tm, POOL_WIDTH), lambda b, i: (b, i, 0)),
            pl.BlockSpec((1, POOL_HALO, POOL_WIDTH),
                         lambda b, i: (b, jnp.maximum(i * nb - 1, 0), 0)),
            pl.BlockSpec((1, POOL_HALO, POOL_WIDTH),
                         lambda b, i: (b, jnp.minimum((i + 1) * nb, nblk - 1), 0)),
            pl.BlockSpec((1, tm, D_MODEL), lambda b, i: (b, i, 0)),
            pl.BlockSpec(w_pool.shape, lambda b, i: (0, 0, 0)),
            pl.BlockSpec((1, POOL_WIDTH), lambda b, i: (0, 0)),
            pl.BlockSpec(w_out.shape, lambda b, i: (0, 0)),
            pl.BlockSpec((1, D_MODEL), lambda b, i: (0, 0)),
        ],
        out_specs=[
            pl.BlockSpec((1, tm, D_MODEL), lambda b, i: (b, i, 0)),
            pl.BlockSpec((1, tm, D_MODEL), lambda b, i: (b, i, 0)),
        ],
        out_shape=[
            jax.ShapeDtypeStruct((B, L, D_MODEL), F32),
            jax.ShapeDtypeStruct((B, L, D_MODEL), BF16),
        ],
        compiler_params=_params("parallel", "parallel"),
        name="mix_out",
    )(a, u, u, u, x, w_pool, pool_scale, w_out, g_ffn)


def _ffn_kernel(h_ref, hprev_ref, hnext_ref, x1_hbm, wg_ref, wv_ref, wc_ref, bc_ref, wd_ref,
                o_ref, hext_ref, act_ref, x1_sem):
    b = pl.program_id(0)
    i = pl.program_id(1)
    j = pl.program_id(2)
    last_i = pl.num_programs(1) - 1
    nf = pl.num_programs(2) - 1
    tm = h_ref.shape[1]
    hal = CONV_HALO

    def x1_copy():
        row0 = pl.multiple_of(i * tm, tm)
        return pltpu.make_async_copy(x1_hbm.at[b, pl.ds(row0, tm), :], o_ref.at[0], x1_sem)

    def up():
        ge = jnp.dot(hext_ref[...], wg_ref[...], preferred_element_type=F32)
        val = jnp.dot(h_ref[0], wv_ref[...], preferred_element_type=F32)
        n = ge.shape[0]
        prev = pltpu.roll(ge, 1, 0)[hal:hal + tm]
        nxt = pltpu.roll(ge, n - 1, 0)[hal:hal + tm]
        tf = ge.shape[1]
        cols = pl.ds(pl.multiple_of(j * tf, tf), tf)
        gate = (prev * wc_ref[0:1, cols] + ge[hal:hal + tm] * wc_ref[1:2, cols]
                + nxt * wc_ref[2:3, cols] + bc_ref[:, cols])
        act_ref[...] = (gate * (1.0 / (1.0 + jnp.exp(-gate))) * val).astype(BF16)

    def down():
        o_ref[0] += jnp.dot(act_ref[...], wd_ref[...], preferred_element_type=F32)

    @pl.when(j == 0)
    def _():
        hext_ref[0:hal, :] = jnp.where(i > 0, hprev_ref[0], jnp.zeros_like(hprev_ref[0]))
        hext_ref[hal:hal + tm, :] = h_ref[0]
        hext_ref[hal + tm:, :] = jnp.where(i < last_i, hnext_ref[0], jnp.zeros_like(hnext_ref[0]))
        x1_copy().start()
        up()

    @pl.when(j == 1)
    def _():
        x1_copy().wait()

    @pl.when((j > 0) & (j < nf))
    def _():
        down()
        up()

    @pl.when(j == nf)
    def _():
        down()


def _ffn(h2, x1, w_up, w_conv, b_conv, w_down, tm, tf):
    B, L, _ = x1.shape
    nb = tm // CONV_HALO
    nblk = L // CONV_HALO
    nf = D_FF // tf
    up_j = lambda j: jnp.minimum(j, nf - 1)
    down_j = lambda j: jnp.maximum(j - 1, 0)
    return pl.pallas_call(
        _ffn_kernel,
        grid=(B, L // tm, nf + 1),
        in_specs=[
            pl.BlockSpec((1, tm, D_MODEL), lambda b, i, j: (b, i, 0)),
            pl.BlockSpec((1, CONV_HALO, D_MODEL),
                         lambda b, i, j: (b, jnp.maximum(i * nb - 1, 0), 0)),
            pl.BlockSpec((1, CONV_HALO, D_MODEL),
                         lambda b, i, j: (b, jnp.minimum((i + 1) * nb, nblk - 1), 0)),
            pl.BlockSpec(memory_space=pl.ANY),
            pl.BlockSpec((D_MODEL, tf), lambda b, i, j: (0, up_j(j))),
            pl.BlockSpec((D_MODEL, tf), lambda b, i, j: (0, up_j(j) + nf)),
            pl.BlockSpec(w_conv.shape, lambda b, i, j: (0, 0)),
            pl.BlockSpec(b_conv.shape, lambda b, i, j: (0, 0)),
            pl.BlockSpec((tf, D_MODEL), lambda b, i, j: (down_j(j), 0)),
        ],
        out_specs=pl.BlockSpec((1, tm, D_MODEL), lambda b, i, j: (b, i, 0)),
        out_shape=jax.ShapeDtypeStruct((B, L, D_MODEL), F32),
        scratch_shapes=[
            pltpu.VMEM((tm + 2 * CONV_HALO, D_MODEL), BF16),
            pltpu.VMEM((tm, tf), BF16),
            pltpu.SemaphoreType.DMA,
        ],
        compiler_params=_params("arbitrary", "arbitrary", "arbitrary"),
        name="ffn",
    )(h2, h2, h2, x1, w_up, w_up, w_conv, b_conv, w_down)


def _tiles(L):
    tm = min(512, L)
    tq = min(256, L)
    tk = min(512, L)
    tf = 512
    attn_unroll = 8
    tm_ffn = min(1024, L)
    ts = min(256, tm)
    return tm, tq, tk, tf, attn_unroll, tm_ffn, ts


LATE_WEIGHTS = ("w_out", "w_up", "w_down")


def _layer(x, p):
    B, L, _ = x.shape
    tm, tq, tk, tf, attn_unroll, tm_ffn, ts = _tiles(L)
    tables = _rope_tables(L)
    pending = [n for n in LATE_WEIGHTS if p[n].dtype != BF16]
    qt, k, vt, u, *cast = _proj(x, p["g_mix"], p["w_in"], p["gq"], p["gk"], tables, tm, ts,
                                cast=[p[n] for n in pending])
    p = {**p, **dict(zip(pending, cast))}
    a = _attention(qt, k, vt, tq, tk, attn_unroll)
    x1, h2 = _mix(a, u, x, p["w_pool"], p["pool_scale"], p["w_out"], p["g_ffn"], tm, ts)
    return _ffn(h2, x1, p["w_up"], p["w_conv"], p["b_conv"], p["w_down"], tm_ffn, tf), p


def _prep(i, g_norm_mix, w_in, g_q, g_k, w_pool, pool_scale, w_out, g_norm_ffn, w_up, w_conv,
          b_conv, w_down):
    q_scale = HEAD_DIM ** -0.5 * math.log2(math.e)
    return dict(
        g_mix=g_norm_mix[i][None, :],
        w_in=w_in[i].astype(BF16),
        gq=(g_q[i] * q_scale)[None, :],
        gk=g_k[i][None, :],
        w_pool=w_pool[i].astype(BF16),
        pool_scale=pool_scale[i][None, :],
        w_out=w_out[i],
        g_ffn=g_norm_ffn[i][None, :],
        w_up=w_up[i],
        w_conv=w_conv[i],
        b_conv=b_conv[i][None, :],
        w_down=w_down[i],
    )


def kernel(x_prompt, x_sample, g_norm_mix, w_in, g_q, g_k, w_pool, pool_scale, w_out,
           g_norm_ffn, w_up, w_conv, b_conv, w_down):
    depth = w_in.shape[0]
    xs = [x_prompt, x_sample]
    for i in range(depth):
        p = _prep(i, g_norm_mix, w_in, g_q, g_k, w_pool, pool_scale, w_out, g_norm_ffn,
                  w_up, w_conv, b_conv, w_down)
        for g in range(len(xs)):
            xs[g], p = _layer(xs[g], p)
    return tuple(xs)
```

```python
import functools
import math

import jax
import jax.numpy as jnp
from jax import lax
from jax.experimental import pallas as pl
from jax.experimental.pallas import tpu as pltpu

D_MODEL = 2048
GRID_W = 64
HEAD_DIM = 128
N_Q_HEADS = 8
N_KV_HEADS = 2
Q_PER_KV = N_Q_HEADS // N_KV_HEADS
ATTN_WIDTH = N_Q_HEADS * HEAD_DIM
KV_WIDTH = N_KV_HEADS * HEAD_DIM
POOL_WINDOWS = (2, 4, 8, 16)
POOL_GROUP_WIDTH = 256
POOL_WIDTH = len(POOL_WINDOWS) * POOL_GROUP_WIDTH
MIX_WIDTH = ATTN_WIDTH + POOL_WIDTH
D_FF = 5632
ROPE_THETA = 10000.0
ROPE_AXIS_DIM = HEAD_DIM // 2
EPS = 1e-6

F32 = jnp.float32
BF16 = jnp.bfloat16

V7X_VMEM_BYTES = 64 * 1024 * 1024
SUBLANES_F32 = 8
SUBLANES_BF16 = 16
POOL_HALO = max(POOL_WINDOWS) // 2
assert POOL_HALO == SUBLANES_F32
CONV_HALO = SUBLANES_BF16

VMEM_LIMIT = 56 * 1024 * 1024


def _params(*sem):
    return pltpu.CompilerParams(dimension_semantics=sem, vmem_limit_bytes=VMEM_LIMIT)


def _rope_tables(L):
    half = ROPE_AXIS_DIM // 2
    inv_freq = ROPE_THETA ** (-jnp.arange(0, ROPE_AXIS_DIM, 2, dtype=F32) / ROPE_AXIS_DIM)

    def axis_tables(n):
        ang = jnp.arange(n, dtype=F32)[:, None] * inv_freq
        cos, sin, zero = jnp.cos(ang), jnp.sin(ang), jnp.zeros((n, half), F32)
        return [jnp.concatenate(t, axis=1) for t in ((cos, cos), (-sin, zero), (zero, sin))]

    by_row = jnp.stack([jnp.pad(t, ((0, 0), (0, ROPE_AXIS_DIM))) for t in axis_tables(L // GRID_W)])
    by_col = jnp.stack([jnp.pad(t, ((0, 0), (ROPE_AXIS_DIM, 0))) for t in axis_tables(GRID_W)])
    return by_row, by_col


def _proj_kernel(x_ref, gmix_ref, win_ref, gq_ref, gk_ref, rrow_ref, rcol_ref, *rest,
                 ts, n_cast):
    cast_in, (qt_ref, k_ref, vt_ref, u_ref), cast_out = (
        rest[:n_cast], rest[n_cast:n_cast + 4], rest[n_cast + 4:])
    for src, dst in zip(cast_in, cast_out):
        dst[...] = src[...].astype(BF16)
    tm = x_ref.shape[1]
    for r0 in range(0, tm, ts):
        rows = slice(r0, r0 + ts)
        x = x_ref[0, rows, :]
        inv = lax.rsqrt(jnp.mean(x * x, axis=-1, keepdims=True) + EPS)
        h = (x * inv * gmix_ref[...]).astype(BF16)
        row0 = (pl.program_id(1) * tm + r0) // GRID_W

        def rope_table(c):
            by_row = jnp.concatenate(
                [jnp.broadcast_to(rrow_ref[c, pl.ds(row0 + q, 1), :], (GRID_W, HEAD_DIM))
                 for q in range(ts // GRID_W)], axis=0)
            return by_row + jnp.concatenate([rcol_ref[c]] * (ts // GRID_W), axis=0)

        cos, sup, sdn = rope_table(0), rope_table(1), rope_table(2)

        def norm_rope(zh, g):
            r = lax.rsqrt(jnp.mean(zh * zh, axis=-1, keepdims=True) + EPS)
            n = zh * r * g
            up = pltpu.roll(n, HEAD_DIM - ROPE_AXIS_DIM // 2, 1)
            dn = pltpu.roll(n, ROPE_AXIS_DIM // 2, 1)
            return n * cos + up * sup + dn * sdn

        zq = jnp.dot(h, win_ref[:, 0:ATTN_WIDTH], preferred_element_type=F32)
        for hh in range(N_Q_HEADS):
            sl = slice(hh * HEAD_DIM, (hh + 1) * HEAD_DIM)
            qt_ref[0, sl, rows] = norm_rope(zq[:, sl], gq_ref[...]).T.astype(BF16)
        zk = jnp.dot(h, win_ref[:, ATTN_WIDTH:ATTN_WIDTH + KV_WIDTH], preferred_element_type=F32)
        for hh in range(N_KV_HEADS):
            sl = slice(hh * HEAD_DIM, (hh + 1) * HEAD_DIM)
            k_ref[0, rows, sl] = norm_rope(zk[:, sl], gk_ref[...]).astype(BF16)
        zv = jnp.dot(h, win_ref[:, ATTN_WIDTH + KV_WIDTH:ATTN_WIDTH + 2 * KV_WIDTH],
                     preferred_element_type=F32)
        for hh in range(N_KV_HEADS):
            sl = slice(hh * HEAD_DIM, (hh + 1) * HEAD_DIM)
            vt_ref[0, sl, rows] = zv[:, sl].T.astype(BF16)
        u_ref[0, rows, :] = jnp.dot(h, win_ref[:, ATTN_WIDTH + 2 * KV_WIDTH:],
                                    preferred_element_type=F32)


def _proj(x, g_mix, w_in, gq, gk, tables, tm, ts, cast=()):
    B, L, _ = x.shape
    nt = L // tm
    rope_by_row, rope_by_col = tables
    assert ts % GRID_W == 0
    whole3 = lambda a: pl.BlockSpec(a.shape, lambda b, i: (0, 0, 0))
    vec = lambda n: pl.BlockSpec((1, n), lambda b, i: (0, 0))
    slab_specs = []
    for w in cast:
        rows, cols = w.shape
        assert rows % (B * nt * SUBLANES_BF16) == 0
        slab_specs.append(pl.BlockSpec((rows // (B * nt), cols), lambda b, i: (b * nt + i, 0)))
    return pl.pallas_call(
        functools.partial(_proj_kernel, ts=ts, n_cast=len(cast)),
        grid=(B, nt),
        in_specs=[
            pl.BlockSpec((1, tm, D_MODEL), lambda b, i: (b, i, 0)),
            vec(D_MODEL),
            pl.BlockSpec(w_in.shape, lambda b, i: (0, 0)),
            vec(HEAD_DIM), vec(HEAD_DIM),
            whole3(rope_by_row), whole3(rope_by_col),
            *slab_specs,
        ],
        out_specs=[
            pl.BlockSpec((1, ATTN_WIDTH, tm), lambda b, i: (b, 0, i)),
            pl.BlockSpec((1, tm, KV_WIDTH), lambda b, i: (b, i, 0)),
            pl.BlockSpec((1, KV_WIDTH, tm), lambda b, i: (b, 0, i)),
            pl.BlockSpec((1, tm, POOL_WIDTH), lambda b, i: (b, i, 0)),
            *slab_specs,
        ],
        out_shape=[
            jax.ShapeDtypeStruct((B, ATTN_WIDTH, L), BF16),
            jax.ShapeDtypeStruct((B, L, KV_WIDTH), BF16),
            jax.ShapeDtypeStruct((B, KV_WIDTH, L), BF16),
            jax.ShapeDtypeStruct((B, L, POOL_WIDTH), F32),
            *[jax.ShapeDtypeStruct(w.shape, BF16) for w in cast],
        ],
        compiler_params=_params("parallel", "parallel"),
        name="proj",
    )(x, g_mix, w_in, gq, gk, rope_by_row, rope_by_col, *cast)


def _attn_kernel(qt_ref, k_ref, vt_ref, o_ref, acc_ref, s_ref, *, tq, tk, unroll):
    L = k_ref.shape[1]
    n = Q_PER_KV * tq
    nk = L // tk
    nq = L // tq
    assert nk % unroll == 0 and unroll % 2 == 0
    ones_rows = (lax.broadcasted_iota(jnp.int32, (SUBLANES_BF16, tk), 0) == 0).astype(BF16)

    def qcat_at(qi):
        c0 = pl.multiple_of(qi * tq, tq)
        return jnp.concatenate(
            [qt_ref[0, g * HEAD_DIM:(g + 1) * HEAD_DIM, pl.ds(c0, tq)] for g in range(Q_PER_KV)],
            axis=1)

    def scores(qc, j):
        start = pl.multiple_of(j * tk, tk)
        return jnp.dot(k_ref[0, pl.ds(start, tk), :], qc,
                       preferred_element_type=F32)

    def issue_scores(slot, qc, j):
        s = scores(qc, j)
        s_ref[slot] = s
        return jnp.max(s, axis=0, keepdims=True)

    def step(j, slot, aslot, m, cmax, next_q, next_j):
        cmax_next = issue_scores(1 - slot, next_q, next_j)
        m_new = jnp.maximum(m, cmax)
        alpha = jnp.exp2(m - m_new)
        pt = jnp.exp2(s_ref[slot] - m_new).astype(BF16)
        start = pl.multiple_of(j * tk, tk)
        vc = jnp.concatenate([vt_ref[0, :, pl.ds(start, tk)], ones_rows], axis=0)
        acc_ref[aslot] = alpha * acc_ref[aslot] + jnp.dot(vc, pt, preferred_element_type=F32)
        return m_new, cmax_next

    def finish(qi, aslot):
        out_t = acc_ref[aslot, 0:HEAD_DIM, :] / acc_ref[aslot, HEAD_DIM:HEAD_DIM + 1, :]
        r0 = pl.multiple_of(qi * tq, tq)
        for g in range(Q_PER_KV):
            o_ref[0, pl.ds(r0, tq), g * HEAD_DIM:(g + 1) * HEAD_DIM] = (
                out_t[:, g * tq:(g + 1) * tq].T.astype(BF16))

    def tile(qi, cmax):
        qc = qcat_at(qi)
        aslot = lax.rem(qi, 2)
        acc_ref[aslot] = jnp.zeros(acc_ref.shape[1:], F32)

        def group(jj, carry):
            for si in range(unroll):
                j = jj * unroll + si
                carry = step(j, si % 2, aslot, *carry, qc, j + 1)
            return carry

        carry = lax.fori_loop(0, nk // unroll - 1, group,
                              (jnp.full((1, n), -jnp.inf, F32), cmax))
        for si in range(unroll - 1):
            j = nk - unroll + si
            carry = step(j, si % 2, aslot, *carry, qc, j + 1)
            if si == 0:
                finish(jnp.maximum(qi - 1, 0), 1 - aslot)
        _, cmax = step(nk - 1, 1, aslot, *carry, qcat_at(jnp.minimum(qi + 1, nq - 1)), 0)
        return cmax

    acc_ref[1] = jnp.ones(acc_ref.shape[1:], F32)
    lax.fori_loop(0, nq, tile, issue_scores(0, qcat_at(0), 0))
    finish(nq - 1, (nq - 1) % 2)


def _attention(qt, k, vt, tq, tk, unroll):
    B, _, L = qt.shape
    gw = Q_PER_KV * HEAD_DIM
    return pl.pallas_call(
        functools.partial(_attn_kernel, tq=tq, tk=tk, unroll=unroll),
        grid=(B, N_KV_HEADS),
        in_specs=[
            pl.BlockSpec((1, gw, L), lambda b, kh: (b, kh, 0)),
            pl.BlockSpec((1, L, HEAD_DIM), lambda b, kh: (b, 0, kh)),
            pl.BlockSpec((1, HEAD_DIM, L), lambda b, kh: (b, kh, 0)),
        ],
        out_specs=pl.BlockSpec((1, L, gw), lambda b, kh: (b, 0, kh)),
        out_shape=jax.ShapeDtypeStruct((B, L, ATTN_WIDTH), BF16),
        scratch_shapes=[
            pltpu.VMEM((2, HEAD_DIM + SUBLANES_BF16, Q_PER_KV * tq), F32),
            pltpu.VMEM((2, tk, Q_PER_KV * tq), F32),
        ],
        compiler_params=_params("parallel", "parallel"),
        name="attention",
    )(qt, k, vt)


def _mix_kernel(a_ref, u_ref, uprev_ref, unext_ref, x_ref, wpool_ref, pscale_ref, wout_ref,
                gffn_ref, x1_ref, h2_ref, *, seq_len, ts):
    i = pl.program_id(1)
    last = pl.num_programs(1) - 1
    tm = u_ref.shape[1]
    hal = POOL_HALO

    def pool(r0, after_value=None):
        rows = slice(r0, r0 + ts)
        tie = None
        if after_value is not None:
            bits = lax.bitcast_convert_type(after_value, jnp.uint32)
            tie = lax.shift_right_logical(lax.shift_right_logical(bits, jnp.uint32(16)),
                                          jnp.uint32(16))
        t = i * tm + r0 + lax.broadcasted_iota(jnp.int32, (ts, 1), 0)
        pooled = []
        for g, w in enumerate(POOL_WINDOWS):
            c0 = g * POOL_GROUP_WIDTH
            cols = slice(c0, c0 + POOL_GROUP_WIDTH)
            before = (u_ref[0, r0 - hal:r0, cols] if r0 > 0
                      else jnp.where(i > 0, uprev_ref[0, :, cols], 0.0))
            after = (u_ref[0, r0 + ts:r0 + ts + hal, cols] if r0 + ts < tm
                     else jnp.where(i < last, unext_ref[0, :, cols], 0.0))
            ue = jnp.concatenate([before, u_ref[0, rows, cols], after], axis=0)
            if tie is not None:
                ue = lax.bitcast_convert_type(
                    lax.bitcast_convert_type(ue, jnp.uint32) | tie, F32)
            n = ue.shape[0]
            tot, span = ue, 1
            while span < w:
                tot = tot + pltpu.roll(tot, span, 0)
                span *= 2
            if w // 2 - 1:
                tot = pltpu.roll(tot, n - (w // 2 - 1), 0)
            cnt = (jnp.minimum(t + w // 2, seq_len) - jnp.maximum(t - w // 2, 0)).astype(F32)
            dm = tot[hal:hal + ts] / cnt - ue[hal:hal + ts]
            y = jnp.dot(dm.astype(BF16), wpool_ref[g], preferred_element_type=F32)
            pooled.append((y * pscale_ref[:, cols]).astype(BF16))
        return jnp.concatenate(pooled, axis=1)

    starts = list(range(0, tm, ts))
    pooled = pool(starts[0])
    for s, r0 in enumerate(starts):
        rows = slice(r0, r0 + ts)
        acc = x_ref[0, rows, :] + jnp.dot(a_ref[0, rows, :], wout_ref[0:ATTN_WIDTH, :],
                                          preferred_element_type=F32)
        pooled_next = (pool(starts[s + 1], acc[0:1, 0:POOL_GROUP_WIDTH])
                       if s + 1 < len(starts) else None)
        x1 = acc + jnp.dot(pooled, wout_ref[ATTN_WIDTH:, :], preferred_element_type=F32)
        x1_ref[0, rows, :] = x1
        inv = lax.rsqrt(jnp.mean(x1 * x1, axis=-1, keepdims=True) + EPS)
        h2_ref[0, rows, :] = (x1 * inv * gffn_ref[...]).astype(BF16)
        pooled = pooled_next


def _mix(a, u, x, w_pool, pool_scale, w_out, g_ffn, tm, ts):
    B, L, _ = x.shape
    nb = tm // POOL_HALO
    nblk = L // POOL_HALO
    return pl.pallas_call(
        functools.partial(_mix_kernel, seq_len=L, ts=ts),
        grid=(B, L // tm),
        in_specs=[
            pl.BlockSpec((1, tm, ATTN_WIDTH), lambda b, i: (b, i, 0)),
            pl.BlockSpec((1, tm, POOL_WIDTH), lambda b, i: (b, i, 0)),
            pl.BlockSpec((1, POOL_HALO, POOL_WIDTH),
                         lambda b, i: (b, jnp.maximum(i * nb - 1, 0), 0)),
            pl.BlockSpec((1, POOL_HALO, POOL_WIDTH),
                         lambda b, i: (b, jnp.minimum((i + 1) * nb, nblk - 1), 0)),
            pl.BlockSpec((1, tm, D_MODEL), lambda b, i: (b, i, 0)),
            pl.BlockSpec(w_pool.shape, lambda b, i: (0, 0, 0)),
            pl.BlockSpec((1, POOL_WIDTH), lambda b, i: (0, 0)),
            pl.BlockSpec(w_out.shape, lambda b, i: (0, 0)),
            pl.BlockSpec((1, D_MODEL), lambda b, i: (0, 0)),
        ],
        out_specs=[
            pl.BlockSpec((1, tm, D_MODEL), lambda b, i: (b, i, 0)),
            pl.BlockSpec((1, tm, D_MODEL), lambda b, i: (b, i, 0)),
        ],
        out_shape=[
            jax.ShapeDtypeStruct((B, L, D_MODEL), F32),
            jax.ShapeDtypeStruct((B, L, D_MODEL), BF16),
        ],
        compiler_params=_params("parallel", "parallel"),
        name="mix_out",
    )(a, u, u, u, x, w_pool, pool_scale, w_out, g_ffn)


def _ffn_kernel(h_ref, hprev_ref, hnext_ref, x1_hbm, wg_ref, wv_ref, wc_ref, bc_ref, wd_ref,
                o_ref, hext_ref, act_ref, x1_sem):
    b = pl.program_id(0)
    i = pl.program_id(1)
    j = pl.program_id(2)
    last_i = pl.num_programs(1) - 1
    nf = pl.num_programs(2) - 1
    tm = h_ref.shape[1]
    hal = CONV_HALO

    def x1_copy():
        row0 = pl.multiple_of(i * tm, tm)
        return pltpu.make_async_copy(x1_hbm.at[b, pl.ds(row0, tm), :], o_ref.at[0], x1_sem)

    def up():
        ge = jnp.dot(hext_ref[...], wg_ref[...], preferred_element_type=F32)
        val = jnp.dot(h_ref[0], wv_ref[...], preferred_element_type=F32)
        n = ge.shape[0]
        prev = pltpu.roll(ge, 1, 0)[hal:hal + tm]
        nxt = pltpu.roll(ge, n - 1, 0)[hal:hal + tm]
        tf = ge.shape[1]
        cols = pl.ds(pl.multiple_of(j * tf, tf), tf)
        gate = (prev * wc_ref[0:1, cols] + ge[hal:hal + tm] * wc_ref[1:2, cols]
                + nxt * wc_ref[2:3, cols] + bc_ref[:, cols])
        act_ref[...] = (gate * (1.0 / (1.0 + jnp.exp(-gate))) * val).astype(BF16)

    def down():
        o_ref[0] += jnp.dot(act_ref[...], wd_ref[...], preferred_element_type=F32)

    @pl.when(j == 0)
    def _():
        hext_ref[0:hal, :] = jnp.where(i > 0, hprev_ref[0], jnp.zeros_like(hprev_ref[0]))
        hext_ref[hal:hal + tm, :] = h_ref[0]
        hext_ref[hal + tm:, :] = jnp.where(i < last_i, hnext_ref[0], jnp.zeros_like(hnext_ref[0]))
        x1_copy().start()
        up()

    @pl.when(j == 1)
    def _():
        x1_copy().wait()

    @pl.when((j > 0) & (j < nf))
    def _():
        down()
        up()

    @pl.when(j == nf)
    def _():
        down()


def _ffn(h2, x1, w_up, w_conv, b_conv, w_down, tm, tf):
    B, L, _ = x1.shape
    nb = tm // CONV_HALO
    nblk = L // CONV_HALO
    nf = D_FF // tf
    up_j = lambda j: jnp.minimum(j, nf - 1)
    down_j = lambda j: jnp.maximum(j - 1, 0)
    return pl.pallas_call(
        _ffn_kernel,
        grid=(B, L // tm, nf + 1),
        in_specs=[
            pl.BlockSpec((1, tm, D_MODEL), lambda b, i, j: (b, i, 0)),
            pl.BlockSpec((1, CONV_HALO, D_MODEL),
                         lambda b, i, j: (b, jnp.maximum(i * nb - 1, 0), 0)),
            pl.BlockSpec((1, CONV_HALO, D_MODEL),
                         lambda b, i, j: (b, jnp.minimum((i + 1) * nb, nblk - 1), 0)),
            pl.BlockSpec(memory_space=pl.ANY),
            pl.BlockSpec((D_MODEL, tf), lambda b, i, j: (0, up_j(j))),
            pl.BlockSpec((D_MODEL, tf), lambda b, i, j: (0, up_j(j) + nf)),
            pl.BlockSpec(w_conv.shape, lambda b, i, j: (0, 0)),
            pl.BlockSpec(b_conv.shape, lambda b, i, j: (0, 0)),
            pl.BlockSpec((tf, D_MODEL), lambda b, i, j: (down_j(j), 0)),
        ],
        out_specs=pl.BlockSpec((1, tm, D_MODEL), lambda b, i, j: (b, i, 0)),
        out_shape=jax.ShapeDtypeStruct((B, L, D_MODEL), F32),
        scratch_shapes=[
            pltpu.VMEM((tm + 2 * CONV_HALO, D_MODEL), BF16),
            pltpu.VMEM((tm, tf), BF16),
            pltpu.SemaphoreType.DMA,
        ],
        compiler_params=_params("arbitrary", "arbitrary", "arbitrary"),
        name="ffn",
    )(h2, h2, h2, x1, w_up, w_up, w_conv, b_conv, w_down)


def _tiles(L):
    tm = min(512, L)
    tq = min(256, L)
    tk = min(512, L)
    tf = 512
    attn_unroll = 8
    tm_ffn = min(1024, L)
    ts = min(256, tm)
    ts_mix = min(256, tm)
    return tm, tq, tk, tf, attn_unroll, tm_ffn, ts, ts_mix


LATE_WEIGHTS = ("w_out", "w_up", "w_down")


def _layer(x, p):
    B, L, _ = x.shape
    tm, tq, tk, tf, attn_unroll, tm_ffn, ts, ts_mix = _tiles(L)
    tables = _rope_tables(L)
    pending = [n for n in LATE_WEIGHTS if p[n].dtype != BF16]
    qt, k, vt, u, *cast = _proj(x, p["g_mix"], p["w_in"], p["gq"], p["gk"], tables, tm, ts,
                                cast=[p[n] for n in pending])
    p = {**p, **dict(zip(pending, cast))}
    a = _attention(qt, k, vt, tq, tk, attn_unroll)
    x1, h2 = _mix(a, u, x, p["w_pool"], p["pool_scale"], p["w_out"], p["g_ffn"], tm, ts_mix)
    return _ffn(h2, x1, p["w_up"], p["w_conv"], p["b_conv"], p["w_down"], tm_ffn, tf), p


def _prep(i, g_norm_mix, w_in, g_q, g_k, w_pool, pool_scale, w_out, g_norm_ffn, w_up, w_conv,
          b_conv, w_down):
    q_scale = HEAD_DIM ** -0.5 * math.log2(math.e)
    return dict(
        g_mix=g_norm_mix[i][None, :],
        w_in=w_in[i].astype(BF16),
        gq=(g_q[i] * q_scale)[None, :],
        gk=g_k[i][None, :],
        w_pool=w_pool[i].astype(BF16),
        pool_scale=pool_scale[i][None, :],
        w_out=w_out[i],
        g_ffn=g_norm_ffn[i][None, :],
        w_up=w_up[i],
        w_conv=w_conv[i],
        b_conv=b_conv[i][None, :],
        w_down=w_down[i],
    )


def kernel(x_prompt, x_sample, g_norm_mix, w_in, g_q, g_k, w_pool, pool_scale, w_out,
           g_norm_ffn, w_up, w_conv, b_conv, w_down):
    depth = w_in.shape[0]
    xs = [x_prompt, x_sample]
    for i in range(depth):
        p = _prep(i, g_norm_mix, w_in, g_q, g_k, w_pool, pool_scale, w_out, g_norm_ffn,
                  w_up, w_conv, b_conv, w_down)
        for g in range(len(xs)):
            xs[g], p = _layer(xs[g], p)
    return tuple(xs)
```
